```python
import math
import jax, jax.numpy as jnp
from jax import lax
import numpy as np

D_MODEL = 2048
BATCH = 2
SEQ = 8192
DEPTH = 1

HEAD_DIM = 128
ROPE_FRAC_DIV = 4
ROPE_DIM = HEAD_DIM // ROPE_FRAC_DIV
ROPE_THETA = 500000.0
DSA_HEADS = 8
DSA_W = DSA_HEADS * HEAD_DIM
DSA_Q_RANK = 512
IDX_HEADS = 16
IDX_DIM = 64
IDX_ROPE_DIM = IDX_DIM // ROPE_FRAC_DIV
IDX_TOPK = 256
IDX_Q_BLOCK = 128
MOBA_HEADS = 8
MOBA_W = MOBA_HEADS * HEAD_DIM
MOBA_BLOCK = 256
MOBA_TOPK = 3
MOBA_Q_BLOCK = 32
MEM_LEN = 256
MEM_HEADS = 4
MEM_HEAD_DIM = 128
MEM_W = MEM_HEADS * MEM_HEAD_DIM
D_FF = 4 * D_MODEL
EPS = 1e-6
IN_SIZES = (DSA_Q_RANK, DSA_W, DSA_W, IDX_DIM, IDX_HEADS, MOBA_W, MOBA_W, MOBA_W, D_MODEL, D_MODEL)
IN_COLS = DSA_Q_RANK + 2 * DSA_W + IDX_DIM + IDX_HEADS + 3 * MOBA_W + 2 * D_MODEL

kernel_name = "hybrid_dsa_moba_gated_block"


def _split_points():
    pts, acc = [], 0
    for s in IN_SIZES[:-1]:
        acc += s
        pts.append(acc)
    return pts


def rmsnorm(x, g):
    x32 = x.astype(jnp.float32)
    y = x32 * lax.rsqrt(jnp.mean(x32 * x32, axis=-1, keepdims=True) + EPS)
    return (y * g.astype(jnp.float32)).astype(x.dtype)


def rope_partial(x, pos, rot_dim):
    half = rot_dim // 2
    inv = ROPE_THETA ** (-jnp.arange(half, dtype=jnp.float32) * (2.0 / rot_dim))
    ang = pos.astype(jnp.float32)[..., None] * inv
    cos = jnp.cos(ang)[:, :, None, :]
    sin = jnp.sin(ang)[:, :, None, :]
    xr = x[..., :rot_dim].astype(jnp.float32)
    x1, x2 = xr[..., :half], xr[..., half:]
    rot = jnp.concatenate([x1 * cos - x2 * sin, x2 * cos + x1 * sin], axis=-1).astype(x.dtype)
    return jnp.concatenate([rot, x[..., rot_dim:]], axis=-1)


def dsa_branch(c_q, k, v, k_idx, w_idx, g_cq, w_uq, w_iq, pos):
    B, S, _ = k.shape
    c_q = rmsnorm(c_q, g_cq)
    q = rope_partial((c_q @ w_uq).reshape(B, S, DSA_HEADS, HEAD_DIM), pos, ROPE_DIM)
    q_idx = rope_partial((c_q @ w_iq).reshape(B, S, IDX_HEADS, IDX_DIM), pos, IDX_ROPE_DIM)
    k = rope_partial(k.reshape(B, S, DSA_HEADS, HEAD_DIM), pos, ROPE_DIM)
    v = v.reshape(B, S, DSA_HEADS, HEAD_DIM)
    k_idx = rope_partial(k_idx[:, :, None, :], pos, IDX_ROPE_DIM)[:, :, 0, :]
    w_idx = w_idx.astype(jnp.float32) * (IDX_HEADS ** -0.5 * IDX_DIM ** -0.5)
    n_sel = min(IDX_TOPK, S // 4)
    n_blocks = S // IDX_Q_BLOCK
    key_pos = jnp.arange(S)
    scale = HEAD_DIM ** -0.5

    def block(i):
        start = i * IDX_Q_BLOCK
        qb = lax.dynamic_slice_in_dim(q, start, IDX_Q_BLOCK, axis=1)
        qib = lax.dynamic_slice_in_dim(q_idx, start, IDX_Q_BLOCK, axis=1)
        wb = lax.dynamic_slice_in_dim(w_idx, start, IDX_Q_BLOCK, axis=1)
        t = start + jnp.arange(IDX_Q_BLOCK)
        logits = jnp.einsum('bqhd,bsd->bqhs', qib, k_idx).astype(jnp.float32)
        score = jnp.einsum('bqh,bqhs->bqs', wb, jax.nn.relu(logits))
        causal = key_pos[None, :] <= t[:, None]
        score = jnp.where(causal[None], score, -jnp.inf)
        _, sel = lax.top_k(score, n_sel)
        valid = sel <= t[None, :, None]
        kg = jax.vmap(lambda kk, ii: kk[ii])(k, sel)
        vg = jax.vmap(lambda vv, ii: vv[ii])(v, sel)
        s = jnp.einsum('bqhd,bqnhd->bhqn', qb, kg).astype(jnp.float32) * scale
        s = jnp.where(valid[:, None], s, -jnp.inf)
        p = jax.nn.softmax(s, axis=-1).astype(v.dtype)
        o = jnp.einsum('bhqn,bqnhd->bqhd', p, vg)
        return o.reshape(B, IDX_Q_BLOCK, DSA_W)

    out = lax.map(block, jnp.arange(n_blocks))
    return out.transpose(1, 0, 2, 3).reshape(B, S, DSA_W)


def moba_branch(q, k, v, pos):
    B, S, _ = q.shape
    q = rope_partial(q.reshape(B, S, MOBA_HEADS, HEAD_DIM), pos, ROPE_DIM)
    k = rope_partial(k.reshape(B, S, MOBA_HEADS, HEAD_DIM), pos, ROPE_DIM)
    v = v.reshape(B, S, MOBA_HEADS, HEAD_DIM)
    nb = -(-S // MOBA_BLOCK)
    pad = nb * MOBA_BLOCK - S
    kp = jnp.pad(k, ((0, 0), (0, pad), (0, 0), (0, 0)))
    vp = jnp.pad(v, ((0, 0), (0, pad), (0, 0), (0, 0)))
    kblk = kp.reshape(B, nb, MOBA_BLOCK, MOBA_HEADS, HEAD_DIM)
    vblk = vp.reshape(B, nb, MOBA_BLOCK, MOBA_HEADS, HEAD_DIM)
    kmean = jnp.mean(kblk.astype(jnp.float32), axis=2).astype(k.dtype)
    kbh = kblk.transpose(0, 3, 1, 2, 4)
    vbh = vblk.transpose(0, 3, 1, 2, 4)
    n_sel = min(MOBA_TOPK, nb - 1)
    n_qblocks = S // MOBA_Q_BLOCK
    scale = HEAD_DIM ** -0.5
    gather2 = jax.vmap(jax.vmap(lambda kk, ii: kk[ii]))

    def block(i):
        start = i * MOBA_Q_BLOCK
        qb = lax.dynamic_slice_in_dim(q, start, MOBA_Q_BLOCK, axis=1)
        t = start + jnp.arange(MOBA_Q_BLOCK)
        cur = start // MOBA_BLOCK
        own_k = lax.dynamic_slice_in_dim(kp, cur * MOBA_BLOCK, MOBA_BLOCK, axis=1)
        own_v = lax.dynamic_slice_in_dim(vp, cur * MOBA_BLOCK, MOBA_BLOCK, axis=1)
        own_pos = cur * MOBA_BLOCK + jnp.arange(MOBA_BLOCK)
        s_own = jnp.einsum('bqhd,bkhd->bhqk', qb, own_k).astype(jnp.float32) * scale
        s_own = jnp.where((own_pos[None, :] <= t[:, None])[None, None], s_own, -jnp.inf)
        if n_sel > 0:
            gate = jnp.einsum('bqhd,bnhd->bhqn', qb, kmean).astype(jnp.float32)
            gate = jnp.where(jnp.arange(nb) < cur, gate, -jnp.inf)
            _, sel = lax.top_k(gate, n_sel)
            valid = sel < cur
            kg = gather2(kbh, sel)
            vg = gather2(vbh, sel)
            qbh = qb.transpose(0, 2, 1, 3)
            s_sel = jnp.einsum('bhqd,bhqnkd->bhqnk', qbh, kg).astype(jnp.float32) * scale
            s_sel = jnp.where(valid[..., None], s_sel, -jnp.inf)
            s_sel = s_sel.reshape(B, MOBA_HEADS, MOBA_Q_BLOCK, n_sel * MOBA_BLOCK)
            p = jax.nn.softmax(jnp.concatenate([s_sel, s_own], axis=-1), axis=-1).astype(v.dtype)
            p_sel = p[..., :n_sel * MOBA_BLOCK].reshape(B, MOBA_HEADS, MOBA_Q_BLOCK, n_sel, MOBA_BLOCK)
            p_own = p[..., n_sel * MOBA_BLOCK:]
            o = (jnp.einsum('bhqnk,bhqnkd->bqhd', p_sel, vg)
                 + jnp.einsum('bhqk,bkhd->bqhd', p_own, own_v))
        else:
            p_own = jax.nn.softmax(s_own, axis=-1).astype(v.dtype)
            o = jnp.einsum('bhqk,bkhd->bqhd', p_own, own_v)
        return o.reshape(B, MOBA_Q_BLOCK, MOBA_W)

    out = lax.map(block, jnp.arange(n_qblocks))
    return out.transpose(1, 0, 2, 3).reshape(B, S, MOBA_W)


def mem_cross_attn(h, mem_n, w_q, w_kv, w_o):
    B, S, _ = h.shape
    M = mem_n.shape[1]
    q = (h @ w_q).reshape(B, S, MEM_HEADS, MEM_HEAD_DIM)
    kv = mem_n @ w_kv
    k = kv[..., :MEM_W].reshape(B, M, MEM_HEADS, MEM_HEAD_DIM)
    v = kv[..., MEM_W:].reshape(B, M, MEM_HEADS, MEM_HEAD_DIM)
    s = jnp.einsum('bshd,bmhd->bhsm', q, k).astype(jnp.float32) * (MEM_HEAD_DIM ** -0.5)
    p = jax.nn.softmax(s, axis=-1).astype(v.dtype)
    o = jnp.einsum('bhsm,bmhd->bshd', p, v).reshape(B, S, MEM_W)
    return o @ w_o


def setup_inputs(seed: int = 0) -> dict:
    key = jax.random.key(seed)
    ks = jax.random.split(key, 24)

    def dense(k, shape, fan_in):
        return jax.random.normal(k, shape, jnp.float32) * (fan_in ** -0.5)

    def gain(k, shape):
        return 1.0 + 0.02 * jax.random.normal(k, shape, jnp.float32)

    x = jax.random.normal(ks[0], (BATCH, SEQ, D_MODEL), jnp.float32)
    mem = jax.random.normal(ks[1], (BATCH, MEM_LEN, D_MODEL), jnp.float32)
    offsets = jax.random.randint(ks[2], (BATCH, 1), 0, 1024, jnp.int32)
    positions = (offsets + jnp.arange(SEQ, dtype=jnp.int32)[None, :]).astype(jnp.int32)
    L = DEPTH
    return {
        "x": x,
        "mem": mem,
        "positions": positions,
        "g_mix": gain(ks[3], (L, D_MODEL)),
        "w_in": dense(ks[4], (L, D_MODEL, IN_COLS), D_MODEL),
        "g_cq": gain(ks[5], (L, DSA_Q_RANK)),
        "w_uq": dense(ks[6], (L, DSA_Q_RANK, DSA_W), DSA_Q_RANK),
        "w_iq": dense(ks[7], (L, DSA_Q_RANK, IDX_HEADS * IDX_DIM), DSA_Q_RANK),
        "w_dsa_o": dense(ks[8], (L, DSA_W, D_MODEL), DSA_W),
        "w_moba_o": dense(ks[9], (L, MOBA_W, D_MODEL), MOBA_W),
        "w_out": dense(ks[10], (L, D_MODEL, D_MODEL), D_MODEL),
        "g_mem_q": gain(ks[11], (L, D_MODEL)),
        "g_mem_kv": gain(ks[12], (L, D_MODEL)),
        "w_mem_q": dense(ks[13], (L, D_MODEL, MEM_W), D_MODEL),
        "w_mem_kv": dense(ks[14], (L, D_MODEL, 2 * MEM_W), D_MODEL),
        "w_mem_o": dense(ks[15], (L, MEM_W, D_MODEL), MEM_W),
        "g_ff": gain(ks[16], (L, D_MODEL)),
        "w_ff1": dense(ks[17], (L, D_MODEL, D_FF), D_MODEL),
        "w_ff2": dense(ks[18], (L, D_FF, D_MODEL), D_FF),
        "g_final": gain(ks[19], (D_MODEL,)),
    }


def reference(x, mem, positions, g_mix, w_in, g_cq, w_uq, w_iq, w_dsa_o, w_moba_o, w_out,
              g_mem_q, g_mem_kv, w_mem_q, w_mem_kv, w_mem_o, g_ff, w_ff1, w_ff2, g_final):
    pts = _split_points()
    for l in range(DEPTH):
        h = rmsnorm(x, g_mix[l])
        proj = h @ w_in[l]
        c_q, k_a, v_a, k_idx, w_idx, q_b, k_b, v_b, gl_a, gl_b = jnp.split(proj, pts, axis=-1)
        y_a = dsa_branch(c_q, k_a, v_a, k_idx, w_idx, g_cq[l], w_uq[l], w_iq[l], positions) @ w_dsa_o[l]
        y_b = moba_branch(q_b, k_b, v_b, positions) @ w_moba_o[l]
        merged = jax.nn.sigmoid(gl_a) * y_a + jax.nn.sigmoid(gl_b) * y_b
        x = x + merged @ w_out[l]
        hm = rmsnorm(x, g_mem_q[l])
        mem_n = rmsnorm(mem, g_mem_kv[l])
        x = x + mem_cross_attn(hm, mem_n, w_mem_q[l], w_mem_kv[l], w_mem_o[l])
        hf = rmsnorm(x, g_ff[l])
        x = x + jnp.square(jax.nn.relu(hf @ w_ff1[l])) @ w_ff2[l]
    return rmsnorm(x, g_final)
```

```python
import functools

import jax
import jax.numpy as jnp
from jax import lax
from jax.experimental import pallas as pl
from jax.experimental.pallas import tpu as pltpu

F32 = jnp.float32
BF16 = jnp.bfloat16

D_MODEL = 2048
HEAD_DIM = 128
ROPE_DIM = 32
ROPE_THETA = 500000.0
DSA_HEADS = 8
DSA_W = DSA_HEADS * HEAD_DIM
DSA_Q_RANK = 512
IDX_HEADS = 16
IDX_DIM = 64
IDX_ROPE_DIM = 16
IDX_TOPK = 256
MOBA_HEADS = 8
MOBA_W = MOBA_HEADS * HEAD_DIM
MOBA_BLOCK = 256
MOBA_TOPK = 3
MEM_HEADS = 4
MEM_HEAD_DIM = 128
MEM_W = MEM_HEADS * MEM_HEAD_DIM
D_FF = 4 * D_MODEL
EPS = 1e-6

LANES = 128
NEG = -1e30
INT_MIN = -(2 ** 31)
VMEM_LIMIT = 48 * 1024 * 1024


def _cparams(sem):
    return pltpu.CompilerParams(dimension_semantics=sem, vmem_limit_bytes=VMEM_LIMIT)


def _norm_mm_kernel(*refs, rope_slabs, rope_half, n_mean, has_tab):
    if has_tab:
        x_ref, g_ref, w_ref, tab_ref = refs[:4]
        rest = refs[4:]
    else:
        x_ref, g_ref, w_ref = refs[:3]
        tab_ref = None
        rest = refs[3:]
    if n_mean:
        o_ref, mean_ref, h_scr = rest
    else:
        o_ref, h_scr = rest
        mean_ref = None

    @pl.when(pl.program_id(1) == 0)
    def _():
        x = x_ref[...]
        ms = jnp.mean(x * x, axis=-1, keepdims=True)
        h_scr[...] = (x * lax.rsqrt(ms + EPS) * g_ref[...]).astype(BF16)

    acc = jnp.dot(h_scr[...], w_ref[...], preferred_element_type=F32)
    tm, tn = acc.shape
    for c in range(tn // LANES):
        y = acc[:, c * LANES:(c + 1) * LANES]
        if rope_slabs is not None and rope_slabs[c]:
            cs = tab_ref[:, 0:LANES]
            sm = tab_ref[:, LANES:2 * LANES]
            sp = tab_ref[:, 2 * LANES:3 * LANES]
            y = (y * cs + pltpu.roll(y, LANES - rope_half, 1) * sm
                 + pltpu.roll(y, rope_half, 1) * sp)
        o_ref[:, c * LANES:(c + 1) * LANES] = y.astype(o_ref.dtype)
        if n_mean:
            for r in range(n_mean):
                blk = y[r * MOBA_BLOCK:(r + 1) * MOBA_BLOCK, :]
                mean_ref[r, :, c * LANES:(c + 1) * LANES] = jnp.mean(blk, axis=0, keepdims=True)


def _norm_matmul(x, g, w, *, out_dtype, tm, tn, name, rope_tab=None, rope_slabs=None,
                 rope_half=0, colmean=False):
    m = x.shape[0]
    k, n = w.shape
    assert m % tm == 0 and n % tn == 0 and tn % LANES == 0
    n_mean = tm // MOBA_BLOCK if colmean else 0
    in_specs = [
        pl.BlockSpec((tm, k), lambda i, j: (i, 0)),
        pl.BlockSpec((1, k), lambda i, j: (0, 0)),
        pl.BlockSpec((k, tn), lambda i, j: (0, j)),
    ]
    args = [x, g.reshape(1, k), w]
    if rope_tab is not None:
        in_specs.append(pl.BlockSpec((tm, 3 * LANES), lambda i, j: (i, 0)))
        args.append(rope_tab)
    out_shape = [jax.ShapeDtypeStruct((m, n), out_dtype)]
    out_specs = [pl.BlockSpec((tm, tn), lambda i, j: (i, j))]
    if colmean:
        out_shape.append(jax.ShapeDtypeStruct((m // MOBA_BLOCK, 1, n), F32))
        out_specs.append(pl.BlockSpec((n_mean, 1, tn), lambda i, j: (i, 0, j)))
    kern = functools.partial(_norm_mm_kernel, rope_slabs=rope_slabs, rope_half=rope_half,
                             n_mean=n_mean, has_tab=rope_tab is not None)
    res = pl.pallas_call(
        kern,
        grid=(m // tm, n // tn),
        in_specs=in_specs,
        out_specs=out_specs,
        out_shape=out_shape,
        scratch_shapes=[pltpu.VMEM((tm, k), BF16)],
        compiler_params=_cparams(("parallel", "arbitrary")),
        name=name,
    )(*args)
    return res if colmean else res[0]


def _mm_res_kernel(a_ref, w_ref, r_ref, o_ref):
    o_ref[...] = r_ref[...] + jnp.dot(a_ref[...], w_ref[...], preferred_element_type=F32)


def _matmul_residual(a, w, res, *, tm, tn, name):
    m, k = a.shape
    n = w.shape[1]
    return pl.pallas_call(
        _mm_res_kernel,
        grid=(m // tm, n // tn),
        in_specs=[
            pl.BlockSpec((tm, k), lambda i, j: (i, 0)),
            pl.BlockSpec((k, tn), lambda i, j: (0, j)),
            pl.BlockSpec((tm, tn), lambda i, j: (i, j)),
        ],
        out_specs=pl.BlockSpec((tm, tn), lambda i, j: (i, j)),
        out_shape=jax.ShapeDtypeStruct((m, n), F32),
        compiler_params=_cparams(("parallel", "parallel")),
        name=name,
    )(a, w, res)


IDX_TQ = 128
IDX_TK = 512


def _indexer_kernel(qx_ref, w_ref, kt_ref, bias_ref, key_scr, wb_scr, *, seq):
    i = pl.program_id(1)
    tq = IDX_TQ
    row0 = i * tq
    n_chunks = (row0 + tq + IDX_TK - 1) // IDX_TK
    total_chunks = seq // IDX_TK

    wslab = w_ref[...] * (IDX_HEADS ** -0.5 * IDX_DIM ** -0.5)
    for h in range(IDX_HEADS):
        wb_scr[h] = jnp.broadcast_to(wslab[:, IDX_DIM + h:IDX_DIM + h + 1], (tq, LANES))

    t_idx = row0 + lax.broadcasted_iota(jnp.int32, (tq, IDX_TK), 0)
    lane_idx = lax.broadcasted_iota(jnp.int32, (tq, IDX_TK), 1)

    def score_chunk(kc, carry):
        k0 = pl.multiple_of(kc * IDX_TK, IDX_TK)
        kt = kt_ref[0, :, pl.ds(k0, IDX_TK)]
        acc = jnp.zeros((tq, IDX_TK), F32)
        for h in range(IDX_HEADS):
            qh = qx_ref[:, h * LANES:(h + 1) * LANES]
            logit = jnp.dot(qh, kt, preferred_element_type=F32)
            wb = wb_scr[h]
            wfull = jnp.concatenate([wb] * (IDX_TK // LANES), axis=1)
            acc = acc + jnp.maximum(logit, 0.0) * wfull
        bits = pltpu.bitcast(acc, jnp.int32)
        key = bits ^ ((bits >> 31) & 0x7FFFFFFF)
        causal = (k0 + lane_idx) <= t_idx
        key_scr[:, pl.ds(k0, IDX_TK)] = jnp.where(causal, key, INT_MIN)
        return carry

    lax.fori_loop(0, n_chunks, score_chunk, 0)

    def count(pred_fn):
        def body(kc, cnt):
            k0 = pl.multiple_of(kc * IDX_TK, IDX_TK)
            keys = key_scr[:, pl.ds(k0, IDX_TK)]
            hit = jnp.where(pred_fn(keys, k0 + lane_idx), 1.0, 0.0)
            for c in range(IDX_TK // LANES):
                cnt = cnt + hit[:, c * LANES:(c + 1) * LANES]
            return cnt
        cnt = lax.fori_loop(0, n_chunks, body, jnp.zeros((tq, LANES), F32))
        return jnp.sum(cnt, axis=1, keepdims=True)

    def bit_body(b, c_u):
        bit = lax.shift_left(jnp.int32(1), 31 - b)
        trial_u = c_u | bit
        trial_s = trial_u ^ INT_MIN
        n_ge = count(lambda keys, _: keys >= trial_s)
        return jnp.where(n_ge >= float(IDX_TOPK), trial_u, c_u)

    c_u = lax.fori_loop(0, 32, bit_body, jnp.zeros((tq, 1), jnp.int32))
    thr = c_u ^ INT_MIN

    n_gt = count(lambda keys, _: keys > thr)
    n_eq = count(lambda keys, _: keys == thr)
    need = float(IDX_TOPK) - n_gt
    has_thr = thr != INT_MIN
    excess = jnp.logical_and(has_thr, n_eq > need)
    idx_all = jnp.where(has_thr, jnp.int32(seq), jnp.int32(-1))
    any_excess = jnp.max(jnp.where(excess, 1.0, 0.0)) > 0.0

    def tie_search():
        def jb(b, lo):
            bit = lax.shift_left(jnp.int32(1), 12 - b)
            trial = lo + bit
            n = count(lambda keys, idx: jnp.logical_and(keys == thr, idx <= trial - 1))
            return jnp.where(n < need, trial, lo)
        lo = lax.fori_loop(0, 13, jb, jnp.zeros((tq, 1), jnp.int32))
        return jnp.where(excess, lo, idx_all)

    thr_idx = lax.cond(any_excess, tie_search, lambda: idx_all)

    def write_chunk(kc, carry):
        k0 = pl.multiple_of(kc * IDX_TK, IDX_TK)
        keys = key_scr[:, pl.ds(k0, IDX_TK)]
        sel = jnp.logical_or(keys > thr,
                             jnp.logical_and(keys == thr, (k0 + lane_idx) <= thr_idx))
        bias_ref[0, :, pl.ds(k0, IDX_TK)] = jnp.where(sel, 0.0, NEG).astype(BF16)
        return carry

    lax.fori_loop(0, n_chunks, write_chunk, 0)

    def fill_chunk(kc, carry):
        k0 = pl.multiple_of(kc * IDX_TK, IDX_TK)
        bias_ref[0, :, pl.ds(k0, IDX_TK)] = jnp.full((tq, IDX_TK), NEG, BF16)
        return carry

    lax.fori_loop(n_chunks, total_chunks, fill_chunk, 0)


def _dsa_select(qx, p3, kt, *, batch, seq):
    nq = seq // IDX_TQ
    return pl.pallas_call(
        functools.partial(_indexer_kernel, seq=seq),
        grid=(batch, nq),
        in_specs=[
            pl.BlockSpec((IDX_TQ, IDX_HEADS * LANES), lambda b, i: (b * nq + i, 0)),
            pl.BlockSpec((IDX_TQ, LANES), lambda b, i: (b * nq + i, DSA_Q_RANK // LANES)),
            pl.BlockSpec((1, LANES, seq), lambda b, i: (b, 0, 0)),
        ],
        out_specs=pl.BlockSpec((1, IDX_TQ, seq), lambda b, i: (b, i, 0)),
        out_shape=jax.ShapeDtypeStruct((batch, seq, seq), BF16),
        scratch_shapes=[
            pltpu.VMEM((IDX_TQ, seq), jnp.int32),
            pltpu.VMEM((IDX_HEADS, IDX_TQ, LANES), F32),
        ],
        compiler_params=_cparams(("parallel", "parallel")),
        name="dsa_indexer_select",
    )(qx, p3, kt)


def _flash_update(q_ref, k_ref, v_ref, rows, bias_fn, m_scr, l_scr, acc_scr, n_heads):
    scale = HEAD_DIM ** -0.5
    for h in range(n_heads):
        sl = slice(h * HEAD_DIM, (h + 1) * HEAD_DIM)
        s = lax.dot_general(q_ref[:, sl], k_ref[rows, sl], (((1,), (1,)), ((), ())),
                            preferred_element_type=F32)
        s = s * scale + bias_fn(h)
        m_prev = m_scr[h]
        m_new = jnp.maximum(m_prev, jnp.max(s, axis=1, keepdims=True))
        alpha = jnp.exp(m_prev - m_new)
        p = jnp.exp(s - m_new)
        l_scr[h] = alpha * l_scr[h] + jnp.sum(p, axis=1, keepdims=True)
        m_scr[h] = m_new
        pv = jnp.dot(p.astype(BF16), v_ref[rows, sl], preferred_element_type=F32)
        acc_scr[:, sl] = alpha * acc_scr[:, sl] + pv


def _flash_init(m_scr, l_scr, acc_scr):
    m_scr[...] = jnp.full(m_scr.shape, -jnp.inf, F32)
    l_scr[...] = jnp.zeros(l_scr.shape, F32)
    acc_scr[...] = jnp.zeros(acc_scr.shape, F32)


def _flash_finish(o_ref, l_scr, acc_scr, n_heads):
    for h in range(n_heads):
        sl = slice(h * HEAD_DIM, (h + 1) * HEAD_DIM)
        o_ref[:, sl] = (acc_scr[:, sl] / l_scr[h]).astype(o_ref.dtype)


DSA_TQ = 256
DSA_TK = 512


def _dsa_attn_kernel(q_ref, k_ref, v_ref, b_ref, o_ref, m_scr, l_scr, acc_scr):
    i = pl.program_id(1)
    kt = pl.program_id(2)
    last = (i * DSA_TQ + DSA_TQ - 1) // DSA_TK

    @pl.when(kt == 0)
    def _():
        _flash_init(m_scr, l_scr, acc_scr)

    @pl.when(kt <= last)
    def _():
        bias = b_ref[0].astype(F32)
        _flash_update(q_ref, k_ref, v_ref, slice(0, DSA_TK), lambda h: bias, m_scr, l_scr,
                      acc_scr, DSA_HEADS)

    @pl.when(kt == last)
    def _():
        _flash_finish(o_ref, l_scr, acc_scr, DSA_HEADS)


def _dsa_attention(q, p1, p2, bias, *, batch, seq):
    nq = seq // DSA_TQ
    nk = seq // DSA_TK

    def last(i):
        return (i * DSA_TQ + DSA_TQ - 1) // DSA_TK

    return pl.pallas_call(
        _dsa_attn_kernel,
        grid=(batch, nq, nk),
        in_specs=[
            pl.BlockSpec((DSA_TQ, DSA_W), lambda b, i, k: (b * nq + i, 0)),
            pl.BlockSpec((DSA_TK, DSA_W), lambda b, i, k: (b * nk + jnp.minimum(k, last(i)), 0)),
            pl.BlockSpec((DSA_TK, DSA_W), lambda b, i, k: (b * nk + jnp.minimum(k, last(i)), 0)),
            pl.BlockSpec((1, DSA_TQ, DSA_TK), lambda b, i, k: (b, i, jnp.minimum(k, last(i)))),
        ],
        out_specs=pl.BlockSpec((DSA_TQ, DSA_W), lambda b, i, k: (b * nq + i, 0)),
        out_shape=jax.ShapeDtypeStruct((batch * seq, DSA_W), BF16),
        scratch_shapes=[
            pltpu.VMEM((DSA_HEADS, DSA_TQ, 1), F32),
            pltpu.VMEM((DSA_HEADS, DSA_TQ, 1), F32),
            pltpu.VMEM((DSA_TQ, DSA_W), F32),
        ],
        compiler_params=_cparams(("parallel", "parallel", "arbitrary")),
        name="dsa_attention",
    )(q, p1, p2, bias)


def _moba_gate_kernel(q_ref, km_ref, rb_ref, *, nb):
    i = pl.program_id(1)
    tq = q_ref.shape[0]
    jidx = lax.broadcasted_iota(jnp.int32, (nb, tq), 0)
    for h in range(MOBA_HEADS):
        sl = slice(h * HEAD_DIM, (h + 1) * HEAD_DIM)
        km = km_ref[:, sl].astype(BF16)
        g = lax.dot_general(km, q_ref[:, sl], (((1,), (1,)), ((), ())),
                            preferred_element_type=F32)
        g = jnp.where(jidx < i, g, -jnp.inf)
        rank = jnp.zeros((nb, tq), F32)
        for jp in range(nb):
            row = g[jp:jp + 1, :]
            tie_ahead = jnp.where(jidx > jp, 1.0, 0.0)
            rank = rank + jnp.where(row > g, 1.0, jnp.where(row == g, tie_ahead, 0.0))
        sel = jnp.logical_and(rank < float(MOBA_TOPK), jidx < i)
        rb_ref[0, h] = jnp.where(sel, 0.0, NEG)


def _moba_gate(p1, kmean, *, batch, seq):
    nb = seq // MOBA_BLOCK
    return pl.pallas_call(
        functools.partial(_moba_gate_kernel, nb=nb),
        grid=(batch, nb),
        in_specs=[
            pl.BlockSpec((MOBA_BLOCK, MOBA_W), lambda b, i: (b * nb + i, 1)),
            pl.BlockSpec((nb, MOBA_W), lambda b, i: (b, 2)),
        ],
        out_specs=pl.BlockSpec((1, MOBA_HEADS, nb, MOBA_BLOCK), lambda b, i: (b, 0, 0, i)),
        out_shape=jax.ShapeDtypeStruct((batch, MOBA_HEADS, nb, seq), F32),
        compiler_params=_cparams(("parallel", "parallel")),
        name="moba_gate",
    )(p1, kmean)


MOBA_TQ = MOBA_BLOCK
MOBA_TK = 2 * MOBA_BLOCK


def _moba_attn_kernel(q_ref, k_ref, v_ref, rb_ref, o_ref, m_scr, l_scr, acc_scr):
    i = pl.program_id(1)
    kt = pl.program_id(2)
    last = i // 2

    @pl.when(kt == 0)
    def _():
        _flash_init(m_scr, l_scr, acc_scr)

    for jj in range(MOBA_TK // MOBA_BLOCK):
        j = kt * (MOBA_TK // MOBA_BLOCK) + jj
        rows = slice(jj * MOBA_BLOCK, (jj + 1) * MOBA_BLOCK)

        @pl.when(j < i)
        def _():
            rb = rb_ref[0, jj]
            _flash_update(q_ref, k_ref, v_ref, rows,
                          lambda h: rb[:, h:h + 1], m_scr, l_scr, acc_scr, MOBA_HEADS)

        @pl.when(j == i)
        def _():
            r = lax.broadcasted_iota(jnp.int32, (MOBA_TQ, MOBA_BLOCK), 0)
            c = lax.broadcasted_iota(jnp.int32, (MOBA_TQ, MOBA_BLOCK), 1)
            causal = jnp.where(c <= r, 0.0, NEG)
            _flash_update(q_ref, k_ref, v_ref, rows,
                          lambda h: causal, m_scr, l_scr, acc_scr, MOBA_HEADS)

    @pl.when(kt == last)
    def _():
        _flash_finish(o_ref, l_scr, acc_scr, MOBA_HEADS)


def _moba_attention(p1, p2, rb, *, batch, seq):
    nq = seq // MOBA_TQ
    nk = seq // MOBA_TK
    sub = MOBA_TK // MOBA_BLOCK

    return pl.pallas_call(
        _moba_attn_kernel,
        grid=(batch, nq, nk),
        in_specs=[
            pl.BlockSpec((MOBA_TQ, MOBA_W), lambda b, i, k: (b * nq + i, 1)),
            pl.BlockSpec((MOBA_TK, MOBA_W), lambda b, i, k: (b * nk + jnp.minimum(k, i // 2), 2)),
            pl.BlockSpec((MOBA_TK, MOBA_W), lambda b, i, k: (b * nk + jnp.minimum(k, i // 2), 1)),
            pl.BlockSpec((1, sub, MOBA_TQ, MOBA_HEADS),
                         lambda b, i, k: (b, jnp.minimum(k, i // 2), i, 0)),
        ],
        out_specs=pl.BlockSpec((MOBA_TQ, MOBA_W), lambda b, i, k: (b * nq + i, 0)),
        out_shape=jax.ShapeDtypeStruct((batch * seq, MOBA_W), BF16),
        scratch_shapes=[
            pltpu.VMEM((MOBA_HEADS, MOBA_TQ, 1), F32),
            pltpu.VMEM((MOBA_HEADS, MOBA_TQ, 1), F32),
            pltpu.VMEM((MOBA_TQ, MOBA_W), F32),
        ],
        compiler_params=_cparams(("parallel", "parallel", "arbitrary")),
        name="moba_attention",
    )(p1, p1, p2, rb)


def _gated_merge_kernel(oa_ref, ob_ref, wa_ref, wb_ref, ga_ref, gb_ref, o_ref):
    ya = jnp.dot(oa_ref[...], wa_ref[...], preferred_element_type=F32)
    yb = jnp.dot(ob_ref[...], wb_ref[...], preferred_element_type=F32)
    sa = 1.0 / (1.0 + jnp.exp(-ga_ref[...]))
    sb = 1.0 / (1.0 + jnp.exp(-gb_ref[...]))
    o_ref[...] = (sa * ya + sb * yb).astype(o_ref.dtype)


def _gated_merge(oa, ob, wa, wb, gates, *, tm, tn):
    m = oa.shape[0]
    n = wa.shape[1]
    nj = n // tn
    return pl.pallas_call(
        _gated_merge_kernel,
        grid=(m // tm, nj),
        in_specs=[
            pl.BlockSpec((tm, DSA_W), lambda i, j: (i, 0)),
            pl.BlockSpec((tm, MOBA_W), lambda i, j: (i, 0)),
            pl.BlockSpec((DSA_W, tn), lambda i, j: (0, j)),
            pl.BlockSpec((MOBA_W, tn), lambda i, j: (0, j)),
            pl.BlockSpec((tm, tn), lambda i, j: (i, j)),
            pl.BlockSpec((tm, tn), lambda i, j: (i, nj + j)),
        ],
        out_specs=pl.BlockSpec((tm, tn), lambda i, j: (i, j)),
        out_shape=jax.ShapeDtypeStruct((m, n), BF16),
        compiler_params=_cparams(("parallel", "parallel")),
        name="gated_merge",
    )(oa, ob, wa, wb, gates, gates)


MEM_TM = 256


def _mem_attn_kernel(x_ref, g_ref, wq_ref, kv_ref, wo_ref, o_ref):
    x = x_ref[...]
    ms = jnp.mean(x * x, axis=-1, keepdims=True)
    hm = (x * lax.rsqrt(ms + EPS) * g_ref[...]).astype(BF16)
    q = jnp.dot(hm, wq_ref[...], preferred_element_type=F32).astype(BF16)
    scale = MEM_HEAD_DIM ** -0.5
    outs = []
    for h in range(MEM_HEADS):
        sl = slice(h * MEM_HEAD_DIM, (h + 1) * MEM_HEAD_DIM)
        kh = kv_ref[0, :, sl]
        vh = kv_ref[0, :, MEM_W + h * MEM_HEAD_DIM:MEM_W + (h + 1) * MEM_HEAD_DIM]
        s = lax.dot_general(q[:, sl], kh, (((1,), (1,)), ((), ())),
                            preferred_element_type=F32) * scale
        e = jnp.exp(s - jnp.max(s, axis=1, keepdims=True))
        p = e / jnp.sum(e, axis=1, keepdims=True)
        outs.append(jnp.dot(p.astype(BF16), vh, preferred_element_type=F32).astype(BF16))
    o = jnp.concatenate(outs, axis=1)
    o_ref[...] = x + jnp.dot(o, wo_ref[...], preferred_element_type=F32)


def _mem_attention(x1, g, wq, kv, wo, *, batch, seq):
    t, d = x1.shape
    mem_len = kv.shape[1]
    per_b = seq // MEM_TM
    return pl.pallas_call(
        _mem_attn_kernel,
        grid=(t // MEM_TM,),
        in_specs=[
            pl.BlockSpec((MEM_TM, d), lambda i: (i, 0)),
            pl.BlockSpec((1, d), lambda i: (0, 0)),
            pl.BlockSpec((d, MEM_W), lambda i: (0, 0)),
            pl.BlockSpec((1, mem_len, 2 * MEM_W), lambda i: (i // per_b, 0, 0)),
            pl.BlockSpec((MEM_W, d), lambda i: (0, 0)),
        ],
        out_specs=pl.BlockSpec((MEM_TM, d), lambda i: (i, 0)),
        out_shape=jax.ShapeDtypeStruct((t, d), F32),
        compiler_params=_cparams(("parallel",)),
        name="mem_cross_attention",
    )(x1, g.reshape(1, d), wq, kv, wo)


FFN_TM = 512
FFN_TF = 512


def _ffn_kernel(x_ref, g_ref, w1_ref, w2_ref, gf_ref, o_ref, h_scr, acc_scr):
    f = pl.program_id(1)

    @pl.when(f == 0)
    def _():
        x = x_ref[...]
        ms = jnp.mean(x * x, axis=-1, keepdims=True)
        h_scr[...] = (x * lax.rsqrt(ms + EPS) * g_ref[...]).astype(BF16)
        acc_scr[...] = jnp.zeros(acc_scr.shape, F32)

    u = jnp.dot(h_scr[...], w1_ref[...], preferred_element_type=F32)
    u = jnp.maximum(u, 0.0)
    u = (u * u).astype(BF16)
    acc_scr[...] += jnp.dot(u, w2_ref[...], preferred_element_type=F32)

    @pl.when(f == pl.num_programs(1) - 1)
    def _():
        y = x_ref[...] + acc_scr[...]
        ms = jnp.mean(y * y, axis=-1, keepdims=True)
        o_ref[...] = y * lax.rsqrt(ms + EPS) * gf_ref[...]


def _ffn_final(x2, g_ff, w1, w2, g_final):
    t, d = x2.shape
    dff = w1.shape[1]
    return pl.pallas_call(
        _ffn_kernel,
        grid=(t // FFN_TM, dff // FFN_TF),
        in_specs=[
            pl.BlockSpec((FFN_TM, d), lambda i, f: (i, 0)),
            pl.BlockSpec((1, d), lambda i, f: (0, 0)),
            pl.BlockSpec((d, FFN_TF), lambda i, f: (0, f)),
            pl.BlockSpec((FFN_TF, d), lambda i, f: (f, 0)),
            pl.BlockSpec((1, d), lambda i, f: (0, 0)),
        ],
        out_specs=pl.BlockSpec((FFN_TM, d), lambda i, f: (i, 0)),
        out_shape=jax.ShapeDtypeStruct((t, d), F32),
        scratch_shapes=[pltpu.VMEM((FFN_TM, d), BF16), pltpu.VMEM((FFN_TM, d), F32)],
        compiler_params=_cparams(("parallel", "arbitrary")),
        name="ffn_final_norm",
    )(x2, g_ff.reshape(1, d), w1, w2, g_final.reshape(1, d))


def _rope_table(pos, rot_dim):
    half = rot_dim // 2
    inv = ROPE_THETA ** (-jnp.arange(half, dtype=F32) * (2.0 / rot_dim))
    ang = pos.astype(F32)[:, None] * inv
    cos, sin = jnp.cos(ang), jnp.sin(ang)
    t = pos.shape[0]
    ones = jnp.ones((t, LANES - rot_dim), F32)
    zeros_h = jnp.zeros((t, half), F32)
    zeros_r = jnp.zeros((t, LANES - rot_dim), F32)
    c = jnp.concatenate([cos, cos, ones], axis=1)
    sm = jnp.concatenate([-sin, zeros_h, zeros_r], axis=1)
    sp = jnp.concatenate([zeros_h, sin, zeros_r], axis=1)
    return jnp.concatenate([c, sm, sp], axis=1)


def kernel(x, mem, positions, g_mix, w_in, g_cq, w_uq, w_iq, w_dsa_o, w_moba_o, w_out,
           g_mem_q, g_mem_kv, w_mem_q, w_mem_kv, w_mem_o, g_ff, w_ff1, w_ff2, g_final):
    batch, seq, d = x.shape
    t = batch * seq
    depth = w_in.shape[0]
    assert depth == 1 and seq % (2 * MOBA_BLOCK) == 0 and d == D_MODEL
    pos = positions.reshape(t)
    tab_head = _rope_table(pos, ROPE_DIM)
    tab_idx = _rope_table(pos, IDX_ROPE_DIM)

    xf = x.reshape(t, d)
    for l in range(depth):
        w = w_in[l]
        o = 0
        parts = {}
        for name, size in (("c_q", DSA_Q_RANK), ("k_a", DSA_W), ("v_a", DSA_W), ("k_idx", IDX_DIM),
                           ("w_idx", IDX_HEADS), ("q_b", MOBA_W), ("k_b", MOBA_W), ("v_b", MOBA_W),
                           ("gl_a", d), ("gl_b", d)):
            parts[name] = w[:, o:o + size]
            o += size
        w_rope = jnp.concatenate([parts["k_a"], parts["q_b"], parts["k_b"]], axis=1).astype(BF16)
        w_v = jnp.concatenate([parts["v_a"], parts["v_b"]], axis=1).astype(BF16)
        pad = jnp.zeros((d, LANES - IDX_DIM - IDX_HEADS), F32)
        w_small = jnp.concatenate([parts["c_q"], parts["k_idx"], parts["w_idx"], pad],
                                  axis=1).astype(BF16)
        w_gate = jnp.concatenate([parts["gl_a"], parts["gl_b"]], axis=1).astype(BF16)
        w_iq_x = jnp.pad(w_iq[l].reshape(DSA_Q_RANK, IDX_HEADS, IDX_DIM),
                         ((0, 0), (0, 0), (0, LANES - IDX_DIM)))
        w_iq_x = w_iq_x.reshape(DSA_Q_RANK, IDX_HEADS * LANES).astype(BF16)

        p1, kmean = _norm_matmul(xf, g_mix[l], w_rope, out_dtype=BF16, tm=512, tn=512,
                                 name="inproj_rope", rope_tab=tab_head, rope_slabs=(True,) * 4,
                                 rope_half=ROPE_DIM // 2, colmean=True)
        kmean = kmean.reshape(t // MOBA_BLOCK, w_rope.shape[1])
        p2 = _norm_matmul(xf, g_mix[l], w_v, out_dtype=BF16, tm=512, tn=512, name="inproj_v")
        n_small = w_small.shape[1]
        p3 = _norm_matmul(xf, g_mix[l], w_small, out_dtype=F32, tm=512, tn=n_small,
                          name="inproj_small", rope_tab=tab_idx,
                          rope_slabs=(False,) * (DSA_Q_RANK // LANES) + (True,),
                          rope_half=IDX_ROPE_DIM // 2)
        gates = _norm_matmul(xf, g_mix[l], w_gate, out_dtype=F32, tm=512, tn=512,
                             name="inproj_gates")
        qx = _norm_matmul(p3, g_cq[l], w_iq_x, out_dtype=BF16, tm=512, tn=1024, name="q_idx_upproj",
                          rope_tab=tab_idx, rope_slabs=(True,) * 8, rope_half=IDX_ROPE_DIM // 2)
        q_a = _norm_matmul(p3, g_cq[l], w_uq[l].astype(BF16), out_dtype=BF16, tm=512, tn=1024,
                           name="q_upproj", rope_tab=tab_head, rope_slabs=(True,) * 8,
                           rope_half=ROPE_DIM // 2)
        kslab = p3[:, DSA_Q_RANK:].reshape(batch, seq, LANES)
        lane = jnp.arange(LANES)
        kt = jnp.where(lane[None, None, :] < IDX_DIM, kslab, 0.0).astype(BF16).transpose(0, 2, 1)
        bias = _dsa_select(qx, p3, kt, batch=batch, seq=seq)
        o_a = _dsa_attention(q_a, p1, p2, bias, batch=batch, seq=seq)
        rb_t = _moba_gate(p1, kmean, batch=batch, seq=seq)
        rb = rb_t.transpose(0, 2, 3, 1)
        o_b = _moba_attention(p1, p2, rb, batch=batch, seq=seq)
        merged = _gated_merge(o_a, o_b, w_dsa_o[l].astype(BF16), w_moba_o[l].astype(BF16), gates,
                              tm=512, tn=512)
        x1 = _matmul_residual(merged, w_out[l].astype(BF16), xf, tm=512, tn=512, name="w_out_res")
        kv = _norm_matmul(mem.reshape(batch * mem.shape[1], d), g_mem_kv[l],
                          w_mem_kv[l].astype(BF16), out_dtype=BF16, tm=256, tn=512, name="mem_kv")
        kv = kv.reshape(batch, mem.shape[1], 2 * MEM_W)
        x2 = _mem_attention(x1, g_mem_q[l], w_mem_q[l].astype(BF16), kv, w_mem_o[l].astype(BF16),
                            batch=batch, seq=seq)
        xf = _ffn_final(x2, g_ff[l], w_ff1[l].astype(BF16), w_ff2[l].astype(BF16), g_final)
    return xf.reshape(batch, seq, d)
```

```python
import functools

import jax
import jax.numpy as jnp
from jax import lax
from jax.experimental import pallas as pl
from jax.experimental.pallas import tpu as pltpu

F32 = jnp.float32
BF16 = jnp.bfloat16

D_MODEL = 2048
HEAD_DIM = 128
ROPE_DIM = 32
ROPE_THETA = 500000.0
DSA_HEADS = 8
DSA_W = DSA_HEADS * HEAD_DIM
DSA_Q_RANK = 512
IDX_HEADS = 16
IDX_DIM = 64
IDX_ROPE_DIM = 16
IDX_TOPK = 256
MOBA_HEADS = 8
MOBA_W = MOBA_HEADS * HEAD_DIM
MOBA_BLOCK = 256
MOBA_TOPK = 3
MEM_HEADS = 4
MEM_HEAD_DIM = 128
MEM_W = MEM_HEADS * MEM_HEAD_DIM
D_FF = 4 * D_MODEL
EPS = 1e-6

LANES = 128
SUBLANES = 8
NEG = -1e30
INT_MIN = -(2 ** 31)
VMEM_LIMIT = 48 * 1024 * 1024

_NT = (((1,), (1,)), ((), ()))
_TN = (((0,), (0,)), ((), ()))


def _cparams(sem):
    return pltpu.CompilerParams(dimension_semantics=sem, vmem_limit_bytes=VMEM_LIMIT)


def _norm_mm_t_kernel(x_ref, g_ref, wt_ref, o_ref, h_scr):
    @pl.when(pl.program_id(1) == 0)
    def _():
        x = x_ref[...]
        ms = jnp.mean(x * x, axis=-1, keepdims=True)
        h_scr[...] = (x * lax.rsqrt(ms + EPS) * g_ref[...]).astype(BF16)

    o_ref[...] = lax.dot_general(wt_ref[...], h_scr[...], _NT,
                                 preferred_element_type=F32).astype(o_ref.dtype)


def _norm_matmul_t(x, g, wt, *, out_dtype, tm, tn, name):
    m, k = x.shape
    n = wt.shape[0]
    return pl.pallas_call(
        _norm_mm_t_kernel,
        grid=(m // tm, n // tn),
        in_specs=[
            pl.BlockSpec((tm, k), lambda i, j: (i, 0)),
            pl.BlockSpec((1, k), lambda i, j: (0, 0)),
            pl.BlockSpec((tn, k), lambda i, j: (j, 0)),
        ],
        out_specs=pl.BlockSpec((tn, tm), lambda i, j: (j, i)),
        out_shape=jax.ShapeDtypeStruct((n, m), out_dtype),
        scratch_shapes=[pltpu.VMEM((tm, k), BF16)],
        compiler_params=_cparams(("parallel", "arbitrary")),
        name=name,
    )(x, g.reshape(1, k), wt)


def _norm_mm_kernel(*refs, rope_slabs, rope_half, n_mean, has_tab):
    if has_tab:
        x_ref, g_ref, w_ref, tab_ref = refs[:4]
        rest = refs[4:]
    else:
        x_ref, g_ref, w_ref = refs[:3]
        tab_ref = None
        rest = refs[3:]
    if n_mean:
        o_ref, mean_ref, h_scr = rest
    else:
        o_ref, h_scr = rest
        mean_ref = None

    @pl.when(pl.program_id(1) == 0)
    def _():
        x = x_ref[...]
        ms = jnp.mean(x * x, axis=-1, keepdims=True)
        h_scr[...] = (x * lax.rsqrt(ms + EPS) * g_ref[...]).astype(BF16)

    acc = jnp.dot(h_scr[...], w_ref[...], preferred_element_type=F32)
    tm, tn = acc.shape
    for c in range(tn // LANES):
        y = acc[:, c * LANES:(c + 1) * LANES]
        if rope_slabs is not None and rope_slabs[c]:
            cs = tab_ref[:, 0:LANES]
            sm = tab_ref[:, LANES:2 * LANES]
            sp = tab_ref[:, 2 * LANES:3 * LANES]
            y = (y * cs + pltpu.roll(y, LANES - rope_half, 1) * sm
                 + pltpu.roll(y, rope_half, 1) * sp)
        o_ref[:, c * LANES:(c + 1) * LANES] = y.astype(o_ref.dtype)
        if n_mean:
            for r in range(n_mean):
                blk = y[r * MOBA_BLOCK:(r + 1) * MOBA_BLOCK, :]
                mean_ref[r, :, c * LANES:(c + 1) * LANES] = jnp.mean(blk, axis=0, keepdims=True)


def _norm_matmul(x, g, w, *, out_dtype, tm, tn, name, rope_tab=None, rope_slabs=None,
                 rope_half=0, colmean=False):
    m = x.shape[0]
    k, n = w.shape
    assert m % tm == 0 and n % tn == 0 and tn % LANES == 0
    n_mean = tm // MOBA_BLOCK if colmean else 0
    in_specs = [
        pl.BlockSpec((tm, k), lambda i, j: (i, 0)),
        pl.BlockSpec((1, k), lambda i, j: (0, 0)),
        pl.BlockSpec((k, tn), lambda i, j: (0, j)),
    ]
    args = [x, g.reshape(1, k), w]
    if rope_tab is not None:
        in_specs.append(pl.BlockSpec((tm, 3 * LANES), lambda i, j: (i, 0)))
        args.append(rope_tab)
    out_shape = [jax.ShapeDtypeStruct((m, n), out_dtype)]
    out_specs = [pl.BlockSpec((tm, tn), lambda i, j: (i, j))]
    if colmean:
        out_shape.append(jax.ShapeDtypeStruct((m // MOBA_BLOCK, 1, n), F32))
        out_specs.append(pl.BlockSpec((n_mean, 1, tn), lambda i, j: (i, 0, j)))
    kern = functools.partial(_norm_mm_kernel, rope_slabs=rope_slabs, rope_half=rope_half,
                             n_mean=n_mean, has_tab=rope_tab is not None)
    res = pl.pallas_call(
        kern,
        grid=(m // tm, n // tn),
        in_specs=in_specs,
        out_specs=out_specs,
        out_shape=out_shape,
        scratch_shapes=[pltpu.VMEM((tm, k), BF16)],
        compiler_params=_cparams(("parallel", "arbitrary")),
        name=name,
    )(*args)
    return res if colmean else res[0]


def _mm_res_kernel(a_ref, w_ref, r_ref, o_ref):
    o_ref[...] = r_ref[...] + jnp.dot(a_ref[...], w_ref[...], preferred_element_type=F32)


def _matmul_residual(a, w, res, *, tm, tn, name):
    m, k = a.shape
    n = w.shape[1]
    return pl.pallas_call(
        _mm_res_kernel,
        grid=(m // tm, n // tn),
        in_specs=[
            pl.BlockSpec((tm, k), lambda i, j: (i, 0)),
            pl.BlockSpec((k, tn), lambda i, j: (0, j)),
            pl.BlockSpec((tm, tn), lambda i, j: (i, j)),
        ],
        out_specs=pl.BlockSpec((tm, tn), lambda i, j: (i, j)),
        out_shape=jax.ShapeDtypeStruct((m, n), F32),
        compiler_params=_cparams(("parallel", "parallel")),
        name=name,
    )(a, w, res)


IDX_TQ = 256
IDX_KC = 512


def _ordered_bits(v):
    return v ^ ((v >> 31) & 0x7FFFFFFF)


def _indexer_kernel(qx_ref, w_ref, kx_ref, bias_ref, sc_scr, *, seq):
    i = pl.program_id(1)
    tq, kc_len = IDX_TQ, IDX_KC
    q0 = i * tq
    n_chunks = (q0 + tq + kc_len - 1) // kc_len
    total_chunks = seq // kc_len

    w_t = (w_ref[...] * (IDX_HEADS ** -0.5 * IDX_DIM ** -0.5)).T
    q_pos = q0 + lax.broadcasted_iota(jnp.int32, (kc_len, tq), 1)
    k_off = lax.broadcasted_iota(jnp.int32, (kc_len, tq), 0)

    def score_chunk(kc, carry):
        k0 = pl.multiple_of(kc * kc_len, kc_len)
        rows = pl.ds(k0, kc_len)
        kx = kx_ref[rows, :]
        for h in range(IDX_HEADS):
            qh = qx_ref[:, h * LANES:(h + 1) * LANES]
            logit = lax.dot_general(kx, qh, _NT, preferred_element_type=F32)
            term = jnp.maximum(logit, 0.0) * w_t[IDX_DIM + h:IDX_DIM + h + 1, :]
            if h == 0:
                sc_scr[rows, :] = term
            elif h < IDX_HEADS - 1:
                sc_scr[rows, :] += term
            else:
                sc_scr[rows, :] = jnp.where(k0 + k_off <= q_pos, sc_scr[rows, :] + term, -jnp.inf)
        return carry

    lax.fori_loop(0, n_chunks, score_chunk, 0)

    def count(pred_fn):
        def body(kc, cnt):
            k0 = pl.multiple_of(kc * kc_len, kc_len)
            blk = sc_scr[pl.ds(k0, kc_len), :]
            hit = jnp.where(pred_fn(blk, k0 + k_off), 1.0, 0.0)
            return cnt + jnp.sum(hit.reshape(kc_len // SUBLANES, SUBLANES, tq), axis=0)
        cnt = lax.fori_loop(0, n_chunks, body, jnp.zeros((SUBLANES, tq), F32))
        return jnp.sum(cnt, axis=0, keepdims=True)

    def bit_body(b, c_u):
        trial_u = c_u | lax.shift_left(jnp.int32(1), 31 - b)
        trial = pltpu.bitcast(_ordered_bits(trial_u ^ INT_MIN), F32)
        n_ge = count(lambda blk, _: blk >= trial)
        return jnp.where(n_ge >= float(IDX_TOPK), trial_u, c_u)

    c_u = lax.fori_loop(0, 32, bit_body, jnp.zeros((1, tq), jnp.int32))
    has_thr = c_u != 0
    thr = jnp.where(has_thr, pltpu.bitcast(_ordered_bits(c_u ^ INT_MIN), F32), -jnp.inf)

    n_gt = count(lambda blk, _: blk > thr)
    n_eq = count(lambda blk, _: blk == thr)
    need = float(IDX_TOPK) - n_gt
    excess = jnp.logical_and(has_thr, n_eq > need)
    idx_all = jnp.where(has_thr, jnp.int32(seq), jnp.int32(-1))
    any_excess = jnp.max(jnp.where(excess, 1.0, 0.0)) > 0.0

    def tie_search():
        def jb(b, lo):
            trial = lo + lax.shift_left(jnp.int32(1), 12 - b)
            n = count(lambda blk, kidx: jnp.logical_and(blk == thr, kidx <= trial - 1))
            return jnp.where(n < need, trial, lo)
        lo = lax.fori_loop(0, 13, jb, jnp.zeros((1, tq), jnp.int32))
        return jnp.where(excess, lo, idx_all)

    thr_idx = lax.cond(any_excess, tie_search, lambda: idx_all)

    def write_chunk(kc, carry):
        k0 = pl.multiple_of(kc * kc_len, kc_len)
        blk = sc_scr[pl.ds(k0, kc_len), :]
        sel = jnp.logical_or(blk > thr, jnp.logical_and(blk == thr, k0 + k_off <= thr_idx))
        bias_ref[0, pl.ds(k0, kc_len), :] = jnp.where(sel, 0.0, NEG).astype(BF16)
        return carry

    lax.fori_loop(0, n_chunks, write_chunk, 0)

    def fill_chunk(kc, carry):
        k0 = pl.multiple_of(kc * kc_len, kc_len)
        bias_ref[0, pl.ds(k0, kc_len), :] = jnp.full((kc_len, tq), NEG, BF16)
        return carry

    lax.fori_loop(n_chunks, total_chunks, fill_chunk, 0)


def _dsa_select(qx, p3, kx, *, batch, seq):
    nq = seq // IDX_TQ
    return pl.pallas_call(
        functools.partial(_indexer_kernel, seq=seq),
        grid=(batch, nq),
        in_specs=[
            pl.BlockSpec((IDX_TQ, IDX_HEADS * LANES), lambda b, i: (b * nq + i, 0)),
            pl.BlockSpec((IDX_TQ, LANES), lambda b, i: (b * nq + i, DSA_Q_RANK // LANES)),
            pl.BlockSpec((seq, LANES), lambda b, i: (b, 0)),
        ],
        out_specs=pl.BlockSpec((1, seq, IDX_TQ), lambda b, i: (b, 0, i)),
        out_shape=jax.ShapeDtypeStruct((batch, seq, seq), BF16),
        scratch_shapes=[pltpu.VMEM((seq, IDX_TQ), F32)],
        compiler_params=_cparams(("parallel", "parallel")),
        name="dsa_indexer_select",
    )(qx, p3, kx)


def _flash_update(q_ref, k_ref, vt_ref, rows, bias_fn, st, n_heads):
    m_scr, l_scr, a_scr, acc_scr, s_scr = st
    scale = HEAD_DIM ** -0.5
    tk = rows.stop - rows.start
    for h in range(n_heads):
        sl = slice(h * HEAD_DIM, (h + 1) * HEAD_DIM)
        s = lax.dot_general(k_ref[rows, sl], q_ref[:, sl], _NT, preferred_element_type=F32)
        s = s * scale + bias_fn(h)
        s_scr[h, 0:tk, :] = s
        m_prev = m_scr[h]
        m_new = jnp.maximum(m_prev, jnp.max(s, axis=0, keepdims=True))
        a_scr[h] = jnp.exp(m_prev - m_new)
        m_scr[h] = m_new
    for h in range(n_heads):
        sl = slice(h * HEAD_DIM, (h + 1) * HEAD_DIM)
        alpha = a_scr[h]
        p = jnp.exp(s_scr[h, 0:tk, :] - m_scr[h])
        l_scr[h] = alpha * l_scr[h] + jnp.sum(p, axis=0, keepdims=True)
        pv = jnp.dot(vt_ref[sl, rows], p.astype(BF16), preferred_element_type=F32)
        acc_scr[sl, :] = alpha * acc_scr[sl, :] + pv


def _flash_init(st):
    m_scr, l_scr, _, acc_scr, _ = st
    m_scr[...] = jnp.full(m_scr.shape, -jnp.inf, F32)
    l_scr[...] = jnp.zeros(l_scr.shape, F32)
    acc_scr[...] = jnp.zeros(acc_scr.shape, F32)


def _flash_finish(o_ref, st, n_heads):
    _, l_scr, _, acc_scr, _ = st
    for h in range(n_heads):
        sl = slice(h * HEAD_DIM, (h + 1) * HEAD_DIM)
        o = acc_scr[sl, :] / l_scr[h]
        o_ref[:, sl] = o.T.astype(o_ref.dtype)


def _flash_scratch(n_heads, tq, tk):
    row = pltpu.VMEM((n_heads, 1, tq), F32)
    return [row, row, row, pltpu.VMEM((n_heads * HEAD_DIM, tq), F32),
            pltpu.VMEM((n_heads, tk, tq), F32)]


DSA_TQ = 256
DSA_TK = 512


def _dsa_attn_kernel(q_ref, k_ref, vt_ref, b_ref, o_ref, *st):
    i = pl.program_id(1)
    kt = pl.program_id(2)
    last = (i * DSA_TQ + DSA_TQ - 1) // DSA_TK

    @pl.when(kt == 0)
    def _():
        _flash_init(st)

    @pl.when(kt <= last)
    def _():
        bias = b_ref[0].astype(F32)
        _flash_update(q_ref, k_ref, vt_ref, slice(0, DSA_TK), lambda h: bias, st, DSA_HEADS)

    @pl.when(kt == last)
    def _():
        _flash_finish(o_ref, st, DSA_HEADS)


def _dsa_attention(q, p1, p2, bias, *, batch, seq):
    nq = seq // DSA_TQ
    nk = seq // DSA_TK

    def last(i):
        return (i * DSA_TQ + DSA_TQ - 1) // DSA_TK

    return pl.pallas_call(
        _dsa_attn_kernel,
        grid=(batch, nq, nk),
        in_specs=[
            pl.BlockSpec((DSA_TQ, DSA_W), lambda b, i, k: (b * nq + i, 0)),
            pl.BlockSpec((DSA_TK, DSA_W), lambda b, i, k: (b * nk + jnp.minimum(k, last(i)), 0)),
            pl.BlockSpec((DSA_W, DSA_TK), lambda b, i, k: (0, b * nk + jnp.minimum(k, last(i)))),
            pl.BlockSpec((1, DSA_TK, DSA_TQ), lambda b, i, k: (b, jnp.minimum(k, last(i)), i)),
        ],
        out_specs=pl.BlockSpec((DSA_TQ, DSA_W), lambda b, i, k: (b * nq + i, 0)),
        out_shape=jax.ShapeDtypeStruct((batch * seq, DSA_W), BF16),
        scratch_shapes=_flash_scratch(DSA_HEADS, DSA_TQ, DSA_TK),
        compiler_params=_cparams(("parallel", "parallel", "arbitrary")),
        name="dsa_attention",
    )(q, p1, p2, bias)


def _moba_gate_kernel(q_ref, km_ref, rb_ref, *, nb):
    i = pl.program_id(1)
    tq = q_ref.shape[0]
    jidx = lax.broadcasted_iota(jnp.int32, (nb, tq), 0)
    for h in range(MOBA_HEADS):
        sl = slice(h * HEAD_DIM, (h + 1) * HEAD_DIM)
        km = km_ref[:, sl].astype(BF16)
        g = lax.dot_general(km, q_ref[:, sl], _NT, preferred_element_type=F32)
        g = jnp.where(jidx < i, g, -jnp.inf)
        rank = jnp.zeros((nb, tq), F32)
        for jp in range(nb):
            row = g[jp:jp + 1, :]
            tie_ahead = jnp.where(jidx > jp, 1.0, 0.0)
            rank = rank + jnp.where(row > g, 1.0, jnp.where(row == g, tie_ahead, 0.0))
        sel = jnp.logical_and(rank < float(MOBA_TOPK), jidx < i)
        rb_ref[0, h] = jnp.where(sel, 0.0, NEG)


def _moba_gate(p1, kmean, *, batch, seq):
    nb = seq // MOBA_BLOCK
    return pl.pallas_call(
        functools.partial(_moba_gate_kernel, nb=nb),
        grid=(batch, nb),
        in_specs=[
            pl.BlockSpec((MOBA_BLOCK, MOBA_W), lambda b, i: (b * nb + i, 1)),
            pl.BlockSpec((nb, MOBA_W), lambda b, i: (b, 2)),
        ],
        out_specs=pl.BlockSpec((1, MOBA_HEADS, nb, MOBA_BLOCK), lambda b, i: (b, 0, 0, i)),
        out_shape=jax.ShapeDtypeStruct((batch, MOBA_HEADS, nb, seq), F32),
        compiler_params=_cparams(("parallel", "parallel")),
        name="moba_gate",
    )(p1, kmean)


MOBA_TQ = MOBA_BLOCK
MOBA_TK = 2 * MOBA_BLOCK


def _moba_attn_kernel(q_ref, k_ref, vt_ref, rb_ref, o_ref, *st):
    i = pl.program_id(1)
    kt = pl.program_id(2)
    last = i // 2
    j0 = 2 * kt

    def row_mask(h, j):
        return jnp.broadcast_to(rb_ref[0, h, pl.ds(j, 1), :], (MOBA_BLOCK, MOBA_TQ))

    @pl.when(kt == 0)
    def _():
        _flash_init(st)

    @pl.when(kt < last)
    def _():
        _flash_update(q_ref, k_ref, vt_ref, slice(0, MOBA_TK),
                      lambda h: jnp.concatenate([row_mask(h, j0), row_mask(h, j0 + 1)], axis=0),
                      st, MOBA_HEADS)

    @pl.when(jnp.logical_and(kt == last, j0 < i))
    def _():
        _flash_update(q_ref, k_ref, vt_ref, slice(0, MOBA_BLOCK), lambda h: row_mask(h, j0), st,
                      MOBA_HEADS)

    for jj in range(2):
        @pl.when(jnp.logical_and(kt == last, j0 + jj == i))
        def _():
            key = lax.broadcasted_iota(jnp.int32, (MOBA_BLOCK, MOBA_TQ), 0)
            qry = lax.broadcasted_iota(jnp.int32, (MOBA_BLOCK, MOBA_TQ), 1)
            causal = jnp.where(key <= qry, 0.0, NEG)
            _flash_update(q_ref, k_ref, vt_ref, slice(jj * MOBA_BLOCK, (jj + 1) * MOBA_BLOCK),
                          lambda h: causal, st, MOBA_HEADS)

    @pl.when(kt == last)
    def _():
        _flash_finish(o_ref, st, MOBA_HEADS)


def _moba_attention(p1, p2, rb, *, batch, seq):
    nq = seq // MOBA_TQ
    nk = seq // MOBA_TK
    nb = seq // MOBA_BLOCK

    return pl.pallas_call(
        _moba_attn_kernel,
        grid=(batch, nq, nk),
        in_specs=[
            pl.BlockSpec((MOBA_TQ, MOBA_W), lambda b, i, k: (b * nq + i, 1)),
            pl.BlockSpec((MOBA_TK, MOBA_W), lambda b, i, k: (b * nk + jnp.minimum(k, i // 2), 2)),
            pl.BlockSpec((MOBA_W, MOBA_TK), lambda b, i, k: (1, b * nk + jnp.minimum(k, i // 2))),
            pl.BlockSpec((1, MOBA_HEADS, nb, MOBA_TQ), lambda b, i, k: (b, 0, 0, i)),
        ],
        out_specs=pl.BlockSpec((MOBA_TQ, MOBA_W), lambda b, i, k: (b * nq + i, 0)),
        out_shape=jax.ShapeDtypeStruct((batch * seq, MOBA_W), BF16),
        scratch_shapes=_flash_scratch(MOBA_HEADS, MOBA_TQ, MOBA_TK),
        compiler_params=_cparams(("parallel", "parallel", "arbitrary")),
        name="moba_attention",
    )(p1, p1, p2, rb)


def _gated_merge_kernel(oa_ref, ob_ref, wa_ref, wb_ref, ga_ref, gb_ref, o_ref):
    ya = jnp.dot(oa_ref[...], wa_ref[...], preferred_element_type=F32)
    yb = jnp.dot(ob_ref[...], wb_ref[...], preferred_element_type=F32)
    sa = 1.0 / (1.0 + jnp.exp(-ga_ref[...]))
    sb = 1.0 / (1.0 + jnp.exp(-gb_ref[...]))
    o_ref[...] = (sa * ya + sb * yb).astype(o_ref.dtype)


def _gated_merge(oa, ob, wa, wb, gates, *, tm, tn):
    m = oa.shape[0]
    n = wa.shape[1]
    nj = n // tn
    return pl.pallas_call(
        _gated_merge_kernel,
        grid=(m // tm, nj),
        in_specs=[
            pl.BlockSpec((tm, DSA_W), lambda i, j: (i, 0)),
            pl.BlockSpec((tm, MOBA_W), lambda i, j: (i, 0)),
            pl.BlockSpec((DSA_W, tn), lambda i, j: (0, j)),
            pl.BlockSpec((MOBA_W, tn), lambda i, j: (0, j)),
            pl.BlockSpec((tm, tn), lambda i, j: (i, j)),
            pl.BlockSpec((tm, tn), lambda i, j: (i, nj + j)),
        ],
        out_specs=pl.BlockSpec((tm, tn), lambda i, j: (i, j)),
        out_shape=jax.ShapeDtypeStruct((m, n), BF16),
        compiler_params=_cparams(("parallel", "parallel")),
        name="gated_merge",
    )(oa, ob, wa, wb, gates, gates)


MEM_TM = 256


def _mem_attn_kernel(x_ref, g_ref, wq_ref, kv_ref, wo_ref, o_ref):
    x = x_ref[...]
    ms = jnp.mean(x * x, axis=-1, keepdims=True)
    hm = (x * lax.rsqrt(ms + EPS) * g_ref[...]).astype(BF16)
    q = jnp.dot(hm, wq_ref[...], preferred_element_type=F32).astype(BF16)
    scale = MEM_HEAD_DIM ** -0.5
    outs = []
    for h in range(MEM_HEADS):
        sl = slice(h * MEM_HEAD_DIM, (h + 1) * MEM_HEAD_DIM)
        kh = kv_ref[0, :, sl]
        vh = kv_ref[0, :, MEM_W + h * MEM_HEAD_DIM:MEM_W + (h + 1) * MEM_HEAD_DIM]
        s = lax.dot_general(q[:, sl], kh, _NT, preferred_element_type=F32) * scale
        e = jnp.exp(s - jnp.max(s, axis=1, keepdims=True))
        p = e / jnp.sum(e, axis=1, keepdims=True)
        outs.append(jnp.dot(p.astype(BF16), vh, preferred_element_type=F32).astype(BF16))
    o = jnp.concatenate(outs, axis=1)
    o_ref[...] = x + jnp.dot(o, wo_ref[...], preferred_element_type=F32)


def _mem_attention(x1, g, wq, kv, wo, *, batch, seq):
    t, d = x1.shape
    mem_len = kv.shape[1]
    per_b = seq // MEM_TM
    return pl.pallas_call(
        _mem_attn_kernel,
        grid=(t // MEM_TM,),
        in_specs=[
            pl.BlockSpec((MEM_TM, d), lambda i: (i, 0)),
            pl.BlockSpec((1, d), lambda i: (0, 0)),
            pl.BlockSpec((d, MEM_W), lambda i: (0, 0)),
            pl.BlockSpec((1, mem_len, 2 * MEM_W), lambda i: (i // per_b, 0, 0)),
            pl.BlockSpec((MEM_W, d), lambda i: (0, 0)),
        ],
        out_specs=pl.BlockSpec((MEM_TM, d), lambda i: (i, 0)),
        out_shape=jax.ShapeDtypeStruct((t, d), F32),
        compiler_params=_cparams(("parallel",)),
        name="mem_cross_attention",
    )(x1, g.reshape(1, d), wq, kv, wo)


FFN_TM = 512
FFN_TF = 512


def _ffn_kernel(x_ref, g_ref, w1_ref, w2_ref, gf_ref, o_ref, h_scr, acc_scr):
    f = pl.program_id(1)

    @pl.when(f == 0)
    def _():
        x = x_ref[...]
        ms = jnp.mean(x * x, axis=-1, keepdims=True)
        h_scr[...] = (x * lax.rsqrt(ms + EPS) * g_ref[...]).astype(BF16)
        acc_scr[...] = jnp.zeros(acc_scr.shape, F32)

    u = jnp.dot(h_scr[...], w1_ref[...], preferred_element_type=F32)
    u = jnp.maximum(u, 0.0)
    u = (u * u).astype(BF16)
    acc_scr[...] += jnp.dot(u, w2_ref[...], preferred_element_type=F32)

    @pl.when(f == pl.num_programs(1) - 1)
    def _():
        y = x_ref[...] + acc_scr[...]
        ms = jnp.mean(y * y, axis=-1, keepdims=True)
        o_ref[...] = y * lax.rsqrt(ms + EPS) * gf_ref[...]


def _ffn_final(x2, g_ff, w1, w2, g_final):
    t, d = x2.shape
    dff = w1.shape[1]
    return pl.pallas_call(
        _ffn_kernel,
        grid=(t // FFN_TM, dff // FFN_TF),
        in_specs=[
            pl.BlockSpec((FFN_TM, d), lambda i, f: (i, 0)),
            pl.BlockSpec((1, d), lambda i, f: (0, 0)),
            pl.BlockSpec((d, FFN_TF), lambda i, f: (0, f)),
            pl.BlockSpec((FFN_TF, d), lambda i, f: (f, 0)),
            pl.BlockSpec((1, d), lambda i, f: (0, 0)),
        ],
        out_specs=pl.BlockSpec((FFN_TM, d), lambda i, f: (i, 0)),
        out_shape=jax.ShapeDtypeStruct((t, d), F32),
        scratch_shapes=[pltpu.VMEM((FFN_TM, d), BF16), pltpu.VMEM((FFN_TM, d), F32)],
        compiler_params=_cparams(("parallel", "arbitrary")),
        name="ffn_final_norm",
    )(x2, g_ff.reshape(1, d), w1, w2, g_final.reshape(1, d))


def _rope_table(pos, rot_dim):
    half = rot_dim // 2
    inv = ROPE_THETA ** (-jnp.arange(half, dtype=F32) * (2.0 / rot_dim))
    ang = pos.astype(F32)[:, None] * inv
    cos, sin = jnp.cos(ang), jnp.sin(ang)
    t = pos.shape[0]
    ones = jnp.ones((t, LANES - rot_dim), F32)
    zeros_h = jnp.zeros((t, half), F32)
    zeros_r = jnp.zeros((t, LANES - rot_dim), F32)
    c = jnp.concatenate([cos, cos, ones], axis=1)
    sm = jnp.concatenate([-sin, zeros_h, zeros_r], axis=1)
    sp = jnp.concatenate([zeros_h, sin, zeros_r], axis=1)
    return jnp.concatenate([c, sm, sp], axis=1)


def kernel(x, mem, positions, g_mix, w_in, g_cq, w_uq, w_iq, w_dsa_o, w_moba_o, w_out,
           g_mem_q, g_mem_kv, w_mem_q, w_mem_kv, w_mem_o, g_ff, w_ff1, w_ff2, g_final):
    batch, seq, d = x.shape
    t = batch * seq
    depth = w_in.shape[0]
    assert depth == 1 and seq % (2 * MOBA_BLOCK) == 0 and d == D_MODEL
    pos = positions.reshape(t)
    tab_head = _rope_table(pos, ROPE_DIM)
    tab_idx = _rope_table(pos, IDX_ROPE_DIM)

    xf = x.reshape(t, d)
    for l in range(depth):
        w = w_in[l]
        o = 0
        parts = {}
        for name, size in (("c_q", DSA_Q_RANK), ("k_a", DSA_W), ("v_a", DSA_W), ("k_idx", IDX_DIM),
                           ("w_idx", IDX_HEADS), ("q_b", MOBA_W), ("k_b", MOBA_W), ("v_b", MOBA_W),
                           ("gl_a", d), ("gl_b", d)):
            parts[name] = w[:, o:o + size]
            o += size
        w_rope = jnp.concatenate([parts["k_a"], parts["q_b"], parts["k_b"]], axis=1).astype(BF16)
        w_v_t = jnp.concatenate([parts["v_a"], parts["v_b"]], axis=1).astype(BF16).T
        pad = jnp.zeros((d, LANES - IDX_DIM - IDX_HEADS), F32)
        w_small = jnp.concatenate([parts["c_q"], parts["k_idx"], parts["w_idx"], pad],
                                  axis=1).astype(BF16)
        w_gate = jnp.concatenate([parts["gl_a"], parts["gl_b"]], axis=1).astype(BF16)
        w_iq_x = jnp.pad(w_iq[l].reshape(DSA_Q_RANK, IDX_HEADS, IDX_DIM),
                         ((0, 0), (0, 0), (0, LANES - IDX_DIM)))
        w_iq_x = w_iq_x.reshape(DSA_Q_RANK, IDX_HEADS * LANES).astype(BF16)

        p1, kmean = _norm_matmul(xf, g_mix[l], w_rope, out_dtype=BF16, tm=512, tn=512,
                                 name="inproj_rope", rope_tab=tab_head, rope_slabs=(True,) * 4,
                                 rope_half=ROPE_DIM // 2, colmean=True)
        kmean = kmean.reshape(t // MOBA_BLOCK, w_rope.shape[1])
        p2 = _norm_matmul_t(xf, g_mix[l], w_v_t, out_dtype=BF16, tm=512, tn=512, name="inproj_v")
        n_small = w_small.shape[1]
        p3 = _norm_matmul(xf, g_mix[l], w_small, out_dtype=F32, tm=512, tn=n_small,
                          name="inproj_small", rope_tab=tab_idx,
                          rope_slabs=(False,) * (DSA_Q_RANK // LANES) + (True,),
                          rope_half=IDX_ROPE_DIM // 2)
        gates = _norm_matmul(xf, g_mix[l], w_gate, out_dtype=F32, tm=512, tn=512,
                             name="inproj_gates")
        qx = _norm_matmul(p3, g_cq[l], w_iq_x, out_dtype=BF16, tm=512, tn=1024, name="q_idx_upproj",
                          rope_tab=tab_idx, rope_slabs=(True,) * 8, rope_half=IDX_ROPE_DIM // 2)
        q_a = _norm_matmul(p3, g_cq[l], w_uq[l].astype(BF16), out_dtype=BF16, tm=512, tn=1024,
                           name="q_upproj", rope_tab=tab_head, rope_slabs=(True,) * 8,
                           rope_half=ROPE_DIM // 2)
        lane = jnp.arange(LANES)
        kx = jnp.where(lane[None, :] < IDX_DIM, p3[:, DSA_Q_RANK:], 0.0).astype(BF16)
        bias = _dsa_select(qx, p3, kx, batch=batch, seq=seq)
        o_a = _dsa_attention(q_a, p1, p2, bias, batch=batch, seq=seq)
        rb = _moba_gate(p1, kmean, batch=batch, seq=seq)
        o_b = _moba_attention(p1, p2, rb, batch=batch, seq=seq)
        merged = _gated_merge(o_a, o_b, w_dsa_o[l].astype(BF16), w_moba_o[l].astype(BF16), gates,
                              tm=512, tn=512)
        x1 = _matmul_residual(merged, w_out[l].astype(BF16), xf, tm=512, tn=512, name="w_out_res")
        kv = _norm_matmul(mem.reshape(batch * mem.shape[1], d), g_mem_kv[l],
                          w_mem_kv[l].astype(BF16), out_dtype=BF16, tm=256, tn=512, name="mem_kv")
        kv = kv.reshape(batch, mem.shape[1], 2 * MEM_W)
        x2 = _mem_attention(x1, g_mem_q[l], w_mem_q[l].astype(BF16), kv, w_mem_o[l].astype(BF16),
                            batch=batch, seq=seq)
        xf = _ffn_final(x2, g_ff[l], w_ff1[l].astype(BF16), w_ff2[l].astype(BF16), g_final)
    return xf.reshape(batch, seq, d)
```

```python
import functools

import jax
import jax.numpy as jnp
from jax import lax
from jax.experimental import pallas as pl
from jax.experimental.pallas import tpu as pltpu

F32 = jnp.float32
BF16 = jnp.bfloat16

D_MODEL = 2048
HEAD_DIM = 128
ROPE_DIM = 32
ROPE_THETA = 500000.0
DSA_HEADS = 8
DSA_W = DSA_HEADS * HEAD_DIM
DSA_Q_RANK = 512
IDX_HEADS = 16
IDX_DIM = 64
IDX_ROPE_DIM = 16
IDX_TOPK = 256
MOBA_HEADS = 8
MOBA_W = MOBA_HEADS * HEAD_DIM
MOBA_BLOCK = 256
MOBA_TOPK = 3
MEM_HEADS = 4
MEM_HEAD_DIM = 128
MEM_W = MEM_HEADS * MEM_HEAD_DIM
D_FF = 4 * D_MODEL
EPS = 1e-6

LANES = 128
SUBLANES = 8
NEG = -1e30
INT_MIN = -(2 ** 31)
VMEM_LIMIT = 48 * 1024 * 1024

LOG2E = 1.4426950408889634
QK_SCALE = HEAD_DIM ** -0.5 * LOG2E

_NT = (((1,), (1,)), ((), ()))
_TN = (((0,), (0,)), ((), ()))


def _cparams(sem):
    return pltpu.CompilerParams(dimension_semantics=sem, vmem_limit_bytes=VMEM_LIMIT)


PROJ_TM = 1024
PROJ_TN = 512


def _rmsnorm_kernel(x_ref, g_ref, o_ref):
    x = x_ref[...]
    ms = jnp.mean(x * x, axis=-1, keepdims=True)
    o_ref[...] = (x * lax.rsqrt(ms + EPS) * g_ref[...]).astype(o_ref.dtype)


def _rmsnorm_cast(x, g, *, tm, name):
    m, k = x.shape
    return pl.pallas_call(
        _rmsnorm_kernel,
        grid=(m // tm,),
        in_specs=[pl.BlockSpec((tm, k), lambda i: (i, 0)), pl.BlockSpec((1, k), lambda i: (0, 0))],
        out_specs=pl.BlockSpec((tm, k), lambda i: (i, 0)),
        out_shape=jax.ShapeDtypeStruct((m, k), BF16),
        compiler_params=_cparams(("parallel",)),
        name=name,
    )(x, g.reshape(1, k))


def _proj_t_kernel(a_ref, wt_ref, o_ref):
    o_ref[...] = lax.dot_general(wt_ref[...], a_ref[...], _NT,
                                 preferred_element_type=F32).astype(o_ref.dtype)


def _project_t(a, wt, *, out_dtype, tm, tn, name):
    m, k = a.shape
    n = wt.shape[0]
    return pl.pallas_call(
        _proj_t_kernel,
        grid=(m // tm, n // tn),
        in_specs=[
            pl.BlockSpec((tm, k), lambda i, j: (i, 0)),
            pl.BlockSpec((tn, k), lambda i, j: (j, 0)),
        ],
        out_specs=pl.BlockSpec((tn, tm), lambda i, j: (j, i)),
        out_shape=jax.ShapeDtypeStruct((n, m), out_dtype),
        compiler_params=_cparams(("parallel", "parallel")),
        name=name,
    )(a, wt)


def _proj_kernel(*refs, norm, rope_slabs, rope_half, n_mean, has_tab, scaled_tiles, scale):
    refs = list(refs)
    x_ref = refs.pop(0)
    g_ref = refs.pop(0) if norm else None
    w_ref = refs.pop(0)
    tab_ref = refs.pop(0) if has_tab else None
    o_ref = refs.pop(0)
    mean_ref = refs.pop(0) if n_mean else None
    h_scr = refs.pop(0) if norm else None
    j = pl.program_id(1)

    if norm:
        @pl.when(j == 0)
        def _():
            x = x_ref[...]
            ms = jnp.mean(x * x, axis=-1, keepdims=True)
            h_scr[...] = (x * lax.rsqrt(ms + EPS) * g_ref[...]).astype(BF16)
        lhs = h_scr[...]
    else:
        lhs = x_ref[...]

    acc = jnp.dot(lhs, w_ref[...], preferred_element_type=F32)
    tm, tn = acc.shape
    if scaled_tiles is not None:
        lo, hi = scaled_tiles
        post = jnp.where(jnp.logical_and(j >= lo, j < hi), scale, 1.0)
    for c in range(tn // LANES):
        y = acc[:, c * LANES:(c + 1) * LANES]
        if rope_slabs is not None and rope_slabs[c]:
            cs = tab_ref[:, 0:LANES]
            sm = tab_ref[:, LANES:2 * LANES]
            sp = tab_ref[:, 2 * LANES:3 * LANES]
            y = (y * cs + pltpu.roll(y, LANES - rope_half, 1) * sm
                 + pltpu.roll(y, rope_half, 1) * sp)
        if n_mean:
            for r in range(n_mean):
                blk = y[r * MOBA_BLOCK:(r + 1) * MOBA_BLOCK, :]
                mean_ref[r, :, c * LANES:(c + 1) * LANES] = jnp.mean(blk, axis=0, keepdims=True)
        if scaled_tiles is not None:
            y = y * post
        o_ref[:, c * LANES:(c + 1) * LANES] = y.astype(o_ref.dtype)


def _project(x, w, *, out_dtype, tm, tn, name, g=None, rope_tab=None, rope_slabs=None,
             rope_half=0, colmean=False, scaled_tiles=None, scale=1.0):
    m = x.shape[0]
    k, n = w.shape
    norm = g is not None
    assert m % tm == 0 and n % tn == 0 and tn % LANES == 0
    n_mean = tm // MOBA_BLOCK if colmean else 0
    in_specs = [pl.BlockSpec((tm, k), lambda i, j: (i, 0))]
    args = [x]
    if norm:
        in_specs.append(pl.BlockSpec((1, k), lambda i, j: (0, 0)))
        args.append(g.reshape(1, k))
    in_specs.append(pl.BlockSpec((k, tn), lambda i, j: (0, j)))
    args.append(w)
    if rope_tab is not None:
        in_specs.append(pl.BlockSpec((tm, 3 * LANES), lambda i, j: (i, 0)))
        args.append(rope_tab)
    out_shape = [jax.ShapeDtypeStruct((m, n), out_dtype)]
    out_specs = [pl.BlockSpec((tm, tn), lambda i, j: (i, j))]
    if colmean:
        out_shape.append(jax.ShapeDtypeStruct((m // MOBA_BLOCK, 1, n), F32))
        out_specs.append(pl.BlockSpec((n_mean, 1, tn), lambda i, j: (i, 0, j)))
    kern = functools.partial(_proj_kernel, norm=norm, rope_slabs=rope_slabs, rope_half=rope_half,
                             n_mean=n_mean, has_tab=rope_tab is not None,
                             scaled_tiles=scaled_tiles, scale=scale)
    res = pl.pallas_call(
        kern,
        grid=(m // tm, n // tn),
        in_specs=in_specs,
        out_specs=out_specs,
        out_shape=out_shape,
        scratch_shapes=[pltpu.VMEM((tm, k), BF16)] if norm else [],
        compiler_params=_cparams(("parallel", "arbitrary" if norm else "parallel")),
        name=name,
    )(*args)
    return res if colmean else res[0]


def _mm_res_kernel(a_ref, w_ref, r_ref, o_ref):
    o_ref[...] = r_ref[...] + jnp.dot(a_ref[...], w_ref[...], preferred_element_type=F32)


def _matmul_residual(a, w, res, *, tm, tn, name):
    m, k = a.shape
    n = w.shape[1]
    return pl.pallas_call(
        _mm_res_kernel,
        grid=(m // tm, n // tn),
        in_specs=[
            pl.BlockSpec((tm, k), lambda i, j: (i, 0)),
            pl.BlockSpec((k, tn), lambda i, j: (0, j)),
            pl.BlockSpec((tm, tn), lambda i, j: (i, j)),
        ],
        out_specs=pl.BlockSpec((tm, tn), lambda i, j: (i, j)),
        out_shape=jax.ShapeDtypeStruct((m, n), F32),
        compiler_params=_cparams(("parallel", "parallel")),
        name=name,
    )(a, w, res)


IDX_TQ = 256
IDX_KC = 512
COUNT_WAYS = 4


def _ordered_bits(v):
    return v ^ ((v >> 31) & 0x7FFFFFFF)


def _indexer_kernel(qx_ref, w_ref, kx_ref, bias_ref, sc_scr, *, seq):
    i = pl.program_id(1)
    tq, kc_len = IDX_TQ, IDX_KC
    q0 = i * tq
    n_chunks = (q0 + tq + kc_len - 1) // kc_len
    total_chunks = seq // kc_len

    w_t = (w_ref[...] * (IDX_HEADS ** -0.5 * IDX_DIM ** -0.5)).T
    q_pos = q0 + lax.broadcasted_iota(jnp.int32, (kc_len, tq), 1)
    k_off = lax.broadcasted_iota(jnp.int32, (kc_len, tq), 0)

    def score_chunk(kc, carry):
        k0 = pl.multiple_of(kc * kc_len, kc_len)
        rows = pl.ds(k0, kc_len)
        kx = kx_ref[rows, :]
        for h in range(IDX_HEADS):
            qh = qx_ref[:, h * LANES:(h + 1) * LANES]
            logit = lax.dot_general(kx, qh, _NT, preferred_element_type=F32)
            term = jnp.maximum(logit, 0.0) * w_t[IDX_DIM + h:IDX_DIM + h + 1, :]
            if h == 0:
                sc_scr[rows, :] = term
            elif h < IDX_HEADS - 1:
                sc_scr[rows, :] += term
            else:
                sc_scr[rows, :] = jnp.where(k0 + k_off <= q_pos, sc_scr[rows, :] + term, -jnp.inf)
        return carry

    lax.fori_loop(0, n_chunks, score_chunk, 0)

    def count(pred_fn):
        def body(kc, cnt):
            k0 = pl.multiple_of(kc * kc_len, kc_len)
            blk = sc_scr[pl.ds(k0, kc_len), :]
            hit = jnp.where(pred_fn(blk, k0 + k_off), 1.0, 0.0)
            hit = hit.reshape(kc_len // (SUBLANES * COUNT_WAYS), COUNT_WAYS, SUBLANES, tq)
            part = hit[0]
            for r in range(1, hit.shape[0]):
                part = part + hit[r]
            return cnt + part
        cnt = lax.fori_loop(0, n_chunks, body, jnp.zeros((COUNT_WAYS, SUBLANES, tq), F32))
        return jnp.sum(jnp.sum(cnt, axis=0), axis=0, keepdims=True)

    def bit_body(b, c_u):
        trial_u = c_u | lax.shift_left(jnp.int32(1), 31 - b)
        trial = pltpu.bitcast(_ordered_bits(trial_u ^ INT_MIN), F32)
        n_ge = count(lambda blk, _: blk >= trial)
        return jnp.where(n_ge >= float(IDX_TOPK), trial_u, c_u)

    c_u = lax.fori_loop(0, 32, bit_body, jnp.zeros((1, tq), jnp.int32))
    has_thr = c_u != 0
    thr = jnp.where(has_thr, pltpu.bitcast(_ordered_bits(c_u ^ INT_MIN), F32), -jnp.inf)

    n_gt = count(lambda blk, _: blk > thr)
    n_eq = count(lambda blk, _: blk == thr)
    need = float(IDX_TOPK) - n_gt
    excess = jnp.logical_and(has_thr, n_eq > need)
    idx_all = jnp.where(has_thr, jnp.int32(seq), jnp.int32(-1))
    any_excess = jnp.max(jnp.where(excess, 1.0, 0.0)) > 0.0

    def tie_search():
        def jb(b, lo):
            trial = lo + lax.shift_left(jnp.int32(1), 12 - b)
            n = count(lambda blk, kidx: jnp.logical_and(blk == thr, kidx <= trial - 1))
            return jnp.where(n < need, trial, lo)
        lo = lax.fori_loop(0, 13, jb, jnp.zeros((1, tq), jnp.int32))
        return jnp.where(excess, lo, idx_all)

    thr_idx = lax.cond(any_excess, tie_search, lambda: idx_all)

    def write_chunk(kc, carry):
        k0 = pl.multiple_of(kc * kc_len, kc_len)
        blk = sc_scr[pl.ds(k0, kc_len), :]
        sel = jnp.logical_or(blk > thr, jnp.logical_and(blk == thr, k0 + k_off <= thr_idx))
        bias_ref[0, pl.ds(k0, kc_len), :] = jnp.where(sel, 0.0, NEG).astype(BF16)
        return carry

    lax.fori_loop(0, n_chunks, write_chunk, 0)

    def fill_chunk(kc, carry):
        k0 = pl.multiple_of(kc * kc_len, kc_len)
        bias_ref[0, pl.ds(k0, kc_len), :] = jnp.full((kc_len, tq), NEG, BF16)
        return carry

    lax.fori_loop(n_chunks, total_chunks, fill_chunk, 0)


def _dsa_select(qx, p3, kx, *, batch, seq):
    nq = seq // IDX_TQ
    return pl.pallas_call(
        functools.partial(_indexer_kernel, seq=seq),
        grid=(batch, nq),
        in_specs=[
            pl.BlockSpec((IDX_TQ, IDX_HEADS * LANES), lambda b, i: (b * nq + i, 0)),
            pl.BlockSpec((IDX_TQ, LANES), lambda b, i: (b * nq + i, DSA_Q_RANK // LANES)),
            pl.BlockSpec((seq, LANES), lambda b, i: (b, 0)),
        ],
        out_specs=pl.BlockSpec((1, seq, IDX_TQ), lambda b, i: (b, 0, i)),
        out_shape=jax.ShapeDtypeStruct((batch, seq, seq), BF16),
        scratch_shapes=[pltpu.VMEM((seq, IDX_TQ), F32)],
        compiler_params=_cparams(("parallel", "parallel")),
        name="dsa_indexer_select",
    )(qx, p3, kx)


def _flash_update(q_ref, k_ref, vt_ref, rows, bias_fn, st, n_heads):
    m_scr, l_scr, a_scr, acc_scr, s_scr = st
    tk = rows.stop - rows.start
    for h in range(n_heads):
        sl = slice(h * HEAD_DIM, (h + 1) * HEAD_DIM)
        s = lax.dot_general(k_ref[rows, sl], q_ref[:, sl], _NT, preferred_element_type=F32)
        s = s + bias_fn(h)
        s_scr[h, 0:tk, :] = s
        m_prev = m_scr[h]
        m_new = jnp.maximum(m_prev, jnp.max(s, axis=0, keepdims=True))
        a_scr[h] = jnp.exp2(m_prev - m_new)
        m_scr[h] = m_new
    for h in range(n_heads):
        sl = slice(h * HEAD_DIM, (h + 1) * HEAD_DIM)
        alpha = a_scr[h]
        p = jnp.exp2(s_scr[h, 0:tk, :] - m_scr[h])
        l_scr[h] = alpha * l_scr[h] + jnp.sum(p, axis=0, keepdims=True)
        pv = jnp.dot(vt_ref[sl, rows], p.astype(BF16), preferred_element_type=F32)
        acc_scr[sl, :] = alpha * acc_scr[sl, :] + pv


def _flash_init(st):
    m_scr, l_scr, _, acc_scr, _ = st
    m_scr[...] = jnp.full(m_scr.shape, -jnp.inf, F32)
    l_scr[...] = jnp.zeros(l_scr.shape, F32)
    acc_scr[...] = jnp.zeros(acc_scr.shape, F32)


def _flash_finish(o_ref, st, n_heads):
    _, l_scr, _, acc_scr, _ = st
    for h in range(n_heads):
        sl = slice(h * HEAD_DIM, (h + 1) * HEAD_DIM)
        o = acc_scr[sl, :] / l_scr[h]
        o_ref[:, sl] = o.T.astype(o_ref.dtype)


def _flash_scratch(n_heads, tq, tk):
    row = pltpu.VMEM((n_heads, 1, tq), F32)
    return [row, row, row, pltpu.VMEM((n_heads * HEAD_DIM, tq), F32),
            pltpu.VMEM((n_heads, tk, tq), F32)]


DSA_TQ = 256
DSA_TK = 512


def _causal_pairs(nq, tq, tk):
    qi, kt = [], []
    for i in range(nq):
        for k in range((i * tq + tq - 1) // tk + 1):
            qi.append(i)
            kt.append(k)
    return jnp.asarray(qi, jnp.int32), jnp.asarray(kt, jnp.int32)


def _dsa_attn_kernel(qi_ref, kt_ref, q_ref, k_ref, vt_ref, b_ref, o_ref, *st):
    s = pl.program_id(1)
    i = qi_ref[s]
    kt = kt_ref[s]
    last = (i * DSA_TQ + DSA_TQ - 1) // DSA_TK

    @pl.when(kt == 0)
    def _():
        _flash_init(st)

    bias = b_ref[0].astype(F32)
    _flash_update(q_ref, k_ref, vt_ref, slice(0, DSA_TK), lambda h: bias, st, DSA_HEADS)

    @pl.when(kt == last)
    def _():
        _flash_finish(o_ref, st, DSA_HEADS)


def _dsa_attention(q, p1, p2, bias, *, batch, seq):
    nq = seq // DSA_TQ
    nk = seq // DSA_TK
    qi, kt = _causal_pairs(nq, DSA_TQ, DSA_TK)
    grid_spec = pltpu.PrefetchScalarGridSpec(
        num_scalar_prefetch=2,
        grid=(batch, qi.shape[0]),
        in_specs=[
            pl.BlockSpec((DSA_TQ, DSA_W), lambda b, s, qi, kt: (b * nq + qi[s], 0)),
            pl.BlockSpec((DSA_TK, DSA_W), lambda b, s, qi, kt: (b * nk + kt[s], 0)),
            pl.BlockSpec((DSA_W, DSA_TK), lambda b, s, qi, kt: (0, b * nk + kt[s])),
            pl.BlockSpec((1, DSA_TK, DSA_TQ), lambda b, s, qi, kt: (b, kt[s], qi[s])),
        ],
        out_specs=pl.BlockSpec((DSA_TQ, DSA_W), lambda b, s, qi, kt: (b * nq + qi[s], 0)),
        scratch_shapes=_flash_scratch(DSA_HEADS, DSA_TQ, DSA_TK),
    )
    return pl.pallas_call(
        _dsa_attn_kernel,
        grid_spec=grid_spec,
        out_shape=jax.ShapeDtypeStruct((batch * seq, DSA_W), BF16),
        compiler_params=_cparams(("parallel", "arbitrary")),
        name="dsa_attention",
    )(qi, kt, q, p1, p2, bias)


def _moba_gate_kernel(q_ref, km_ref, rb_ref, *, nb):
    i = pl.program_id(1)
    tq = q_ref.shape[0]
    jidx = lax.broadcasted_iota(jnp.int32, (nb, tq), 0)
    for h in range(MOBA_HEADS):
        sl = slice(h * HEAD_DIM, (h + 1) * HEAD_DIM)
        km = km_ref[:, sl].astype(BF16)
        g = lax.dot_general(km, q_ref[:, sl], _NT, preferred_element_type=F32)
        g = jnp.where(jidx < i, g, -jnp.inf)
        rank = jnp.zeros((nb, tq), F32)
        for jp in range(nb):
            row = g[jp:jp + 1, :]
            tie_ahead = jnp.where(jidx > jp, 1.0, 0.0)
            rank = rank + jnp.where(row > g, 1.0, jnp.where(row == g, tie_ahead, 0.0))
        sel = jnp.logical_and(rank < float(MOBA_TOPK), jidx < i)
        rb_ref[0, h] = jnp.where(sel, 0.0, NEG)


def _moba_gate(p1, kmean, *, batch, seq):
    nb = seq // MOBA_BLOCK
    return pl.pallas_call(
        functools.partial(_moba_gate_kernel, nb=nb),
        grid=(batch, nb),
        in_specs=[
            pl.BlockSpec((MOBA_BLOCK, MOBA_W), lambda b, i: (b * nb + i, 1)),
            pl.BlockSpec((nb, MOBA_W), lambda b, i: (b, 2)),
        ],
        out_specs=pl.BlockSpec((1, MOBA_HEADS, nb, MOBA_BLOCK), lambda b, i: (b, 0, 0, i)),
        out_shape=jax.ShapeDtypeStruct((batch, MOBA_HEADS, nb, seq), F32),
        compiler_params=_cparams(("parallel", "parallel")),
        name="moba_gate",
    )(p1, kmean)


MOBA_TQ = MOBA_BLOCK
MOBA_TK = 2 * MOBA_BLOCK


def _moba_attn_kernel(qi_ref, kt_ref, q_ref, k_ref, vt_ref, rb_ref, o_ref, *st):
    s = pl.program_id(1)
    i = qi_ref[s]
    kt = kt_ref[s]
    last = i // 2
    j0 = 2 * kt

    def row_mask(h, j):
        return jnp.broadcast_to(rb_ref[0, h, pl.ds(j, 1), :], (MOBA_BLOCK, MOBA_TQ))

    @pl.when(kt == 0)
    def _():
        _flash_init(st)

    @pl.when(kt < last)
    def _():
        _flash_update(q_ref, k_ref, vt_ref, slice(0, MOBA_TK),
                      lambda h: jnp.concatenate([row_mask(h, j0), row_mask(h, j0 + 1)], axis=0),
                      st, MOBA_HEADS)

    @pl.when(jnp.logical_and(kt == last, j0 < i))
    def _():
        _flash_update(q_ref, k_ref, vt_ref, slice(0, MOBA_BLOCK), lambda h: row_mask(h, j0), st,
                      MOBA_HEADS)

    for jj in range(2):
        @pl.when(jnp.logical_and(kt == last, j0 + jj == i))
        def _():
            key = lax.broadcasted_iota(jnp.int32, (MOBA_BLOCK, MOBA_TQ), 0)
            qry = lax.broadcasted_iota(jnp.int32, (MOBA_BLOCK, MOBA_TQ), 1)
            causal = jnp.where(key <= qry, 0.0, NEG)
            _flash_update(q_ref, k_ref, vt_ref, slice(jj * MOBA_BLOCK, (jj + 1) * MOBA_BLOCK),
                          lambda h: causal, st, MOBA_HEADS)

    @pl.when(kt == last)
    def _():
        _flash_finish(o_ref, st, MOBA_HEADS)


def _moba_attention(p1, p2, rb, *, batch, seq):
    nq = seq // MOBA_TQ
    nk = seq // MOBA_TK
    nb = seq // MOBA_BLOCK
    qi, kt = _causal_pairs(nq, MOBA_TQ, MOBA_TK)
    grid_spec = pltpu.PrefetchScalarGridSpec(
        num_scalar_prefetch=2,
        grid=(batch, qi.shape[0]),
        in_specs=[
            pl.BlockSpec((MOBA_TQ, MOBA_W), lambda b, s, qi, kt: (b * nq + qi[s], 1)),
            pl.BlockSpec((MOBA_TK, MOBA_W), lambda b, s, qi, kt: (b * nk + kt[s], 2)),
            pl.BlockSpec((MOBA_W, MOBA_TK), lambda b, s, qi, kt: (1, b * nk + kt[s])),
            pl.BlockSpec((1, MOBA_HEADS, nb, MOBA_TQ), lambda b, s, qi, kt: (b, 0, 0, qi[s])),
        ],
        out_specs=pl.BlockSpec((MOBA_TQ, MOBA_W), lambda b, s, qi, kt: (b * nq + qi[s], 0)),
        scratch_shapes=_flash_scratch(MOBA_HEADS, MOBA_TQ, MOBA_TK),
    )
    return pl.pallas_call(
        _moba_attn_kernel,
        grid_spec=grid_spec,
        out_shape=jax.ShapeDtypeStruct((batch * seq, MOBA_W), BF16),
        compiler_params=_cparams(("parallel", "arbitrary")),
        name="moba_attention",
    )(qi, kt, p1, p1, p2, rb)


def _gated_merge_kernel(h_ref, oa_ref, ob_ref, wga_ref, wgb_ref, wa_ref, wb_ref, o_ref):
    h = h_ref[...]
    ga = jnp.dot(h, wga_ref[...], preferred_element_type=F32)
    ya = jnp.dot(oa_ref[...], wa_ref[...], preferred_element_type=F32)
    out = ya / (1.0 + jnp.exp(-ga))
    gb = jnp.dot(h, wgb_ref[...], preferred_element_type=F32)
    yb = jnp.dot(ob_ref[...], wb_ref[...], preferred_element_type=F32)
    o_ref[...] = (out + yb / (1.0 + jnp.exp(-gb))).astype(o_ref.dtype)


def _gated_merge(h, oa, ob, w_gate, wa, wb, *, tm, tn):
    m, d = h.shape
    n = wa.shape[1]
    nj = n // tn
    return pl.pallas_call(
        _gated_merge_kernel,
        grid=(m // tm, nj),
        in_specs=[
            pl.BlockSpec((tm, d), lambda i, j: (i, 0)),
            pl.BlockSpec((tm, DSA_W), lambda i, j: (i, 0)),
            pl.BlockSpec((tm, MOBA_W), lambda i, j: (i, 0)),
            pl.BlockSpec((d, tn), lambda i, j: (0, j)),
            pl.BlockSpec((d, tn), lambda i, j: (0, nj + j)),
            pl.BlockSpec((DSA_W, tn), lambda i, j: (0, j)),
            pl.BlockSpec((MOBA_W, tn), lambda i, j: (0, j)),
        ],
        out_specs=pl.BlockSpec((tm, tn), lambda i, j: (i, j)),
        out_shape=jax.ShapeDtypeStruct((m, n), BF16),
        compiler_params=_cparams(("parallel", "parallel")),
        name="gated_merge",
    )(h, oa, ob, w_gate, w_gate, wa, wb)


MEM_TM = 256


def _mem_attn_kernel(x_ref, g_ref, wq_ref, kv_ref, wo_ref, o_ref):
    x = x_ref[...]
    ms = jnp.mean(x * x, axis=-1, keepdims=True)
    hm = (x * lax.rsqrt(ms + EPS) * g_ref[...]).astype(BF16)
    q = jnp.dot(hm, wq_ref[...], preferred_element_type=F32).astype(BF16)
    scale = MEM_HEAD_DIM ** -0.5
    outs = []
    for h in range(MEM_HEADS):
        sl = slice(h * MEM_HEAD_DIM, (h + 1) * MEM_HEAD_DIM)
        kh = kv_ref[0, :, sl]
        vh = kv_ref[0, :, MEM_W + h * MEM_HEAD_DIM:MEM_W + (h + 1) * MEM_HEAD_DIM]
        s = lax.dot_general(q[:, sl], kh, _NT, preferred_element_type=F32) * scale
        e = jnp.exp(s - jnp.max(s, axis=1, keepdims=True))
        p = e / jnp.sum(e, axis=1, keepdims=True)
        outs.append(jnp.dot(p.astype(BF16), vh, preferred_element_type=F32).astype(BF16))
    o = jnp.concatenate(outs, axis=1)
    o_ref[...] = x + jnp.dot(o, wo_ref[...], preferred_element_type=F32)


def _mem_attention(x1, g, wq, kv, wo, *, batch, seq):
    t, d = x1.shape
    mem_len = kv.shape[1]
    per_b = seq // MEM_TM
    return pl.pallas_call(
        _mem_attn_kernel,
        grid=(t // MEM_TM,),
        in_specs=[
            pl.BlockSpec((MEM_TM, d), lambda i: (i, 0)),
            pl.BlockSpec((1, d), lambda i: (0, 0)),
            pl.BlockSpec((d, MEM_W), lambda i: (0, 0)),
            pl.BlockSpec((1, mem_len, 2 * MEM_W), lambda i: (i // per_b, 0, 0)),
            pl.BlockSpec((MEM_W, d), lambda i: (0, 0)),
        ],
        out_specs=pl.BlockSpec((MEM_TM, d), lambda i: (i, 0)),
        out_shape=jax.ShapeDtypeStruct((t, d), F32),
        compiler_params=_cparams(("parallel",)),
        name="mem_cross_attention",
    )(x1, g.reshape(1, d), wq, kv, wo)


FFN_TM = 1024
FFN_TF = 512


def _ffn_kernel(x_ref, g_ref, w1_ref, w2_ref, gf_ref, o_ref, h_scr, acc_scr):
    f = pl.program_id(1)

    @pl.when(f == 0)
    def _():
        x = x_ref[...]
        ms = jnp.mean(x * x, axis=-1, keepdims=True)
        h_scr[...] = (x * lax.rsqrt(ms + EPS) * g_ref[...]).astype(BF16)
        acc_scr[...] = jnp.zeros(acc_scr.shape, F32)

    u = jnp.dot(h_scr[...], w1_ref[...], preferred_element_type=F32)
    u = jnp.maximum(u, 0.0)
    u = (u * u).astype(BF16)
    acc_scr[...] += jnp.dot(u, w2_ref[...], preferred_element_type=F32)

    @pl.when(f == pl.num_programs(1) - 1)
    def _():
        y = x_ref[...] + acc_scr[...]
        ms = jnp.mean(y * y, axis=-1, keepdims=True)
        o_ref[...] = y * lax.rsqrt(ms + EPS) * gf_ref[...]


def _ffn_final(x2, g_ff, w1, w2, g_final):
    t, d = x2.shape
    dff = w1.shape[1]
    return pl.pallas_call(
        _ffn_kernel,
        grid=(t // FFN_TM, dff // FFN_TF),
        in_specs=[
            pl.BlockSpec((FFN_TM, d), lambda i, f: (i, 0), pipeline_mode=pl.Buffered(1)),
            pl.BlockSpec((1, d), lambda i, f: (0, 0)),
            pl.BlockSpec((d, FFN_TF), lambda i, f: (0, f)),
            pl.BlockSpec((FFN_TF, d), lambda i, f: (f, 0)),
            pl.BlockSpec((1, d), lambda i, f: (0, 0)),
        ],
        out_specs=pl.BlockSpec((FFN_TM, d), lambda i, f: (i, 0), pipeline_mode=pl.Buffered(1)),
        out_shape=jax.ShapeDtypeStruct((t, d), F32),
        scratch_shapes=[pltpu.VMEM((FFN_TM, d), BF16), pltpu.VMEM((FFN_TM, d), F32)],
        compiler_params=_cparams(("parallel", "arbitrary")),
        name="ffn_final_norm",
    )(x2, g_ff.reshape(1, d), w1, w2, g_final.reshape(1, d))


def _rope_table(pos, rot_dim):
    half = rot_dim // 2
    inv = ROPE_THETA ** (-jnp.arange(half, dtype=F32) * (2.0 / rot_dim))
    ang = pos.astype(F32)[:, None] * inv
    cos, sin = jnp.cos(ang), jnp.sin(ang)
    t = pos.shape[0]
    ones = jnp.ones((t, LANES - rot_dim), F32)
    zeros_h = jnp.zeros((t, half), F32)
    zeros_r = jnp.zeros((t, LANES - rot_dim), F32)
    c = jnp.concatenate([cos, cos, ones], axis=1)
    sm = jnp.concatenate([-sin, zeros_h, zeros_r], axis=1)
    sp = jnp.concatenate([zeros_h, sin, zeros_r], axis=1)
    return jnp.concatenate([c, sm, sp], axis=1)


def kernel(x, mem, positions, g_mix, w_in, g_cq, w_uq, w_iq, w_dsa_o, w_moba_o, w_out,
           g_mem_q, g_mem_kv, w_mem_q, w_mem_kv, w_mem_o, g_ff, w_ff1, w_ff2, g_final):
    batch, seq, d = x.shape
    t = batch * seq
    depth = w_in.shape[0]
    assert depth == 1 and seq % (2 * MOBA_BLOCK) == 0 and d == D_MODEL
    pos = positions.reshape(t)
    tab_head = _rope_table(pos, ROPE_DIM)
    tab_idx = _rope_table(pos, IDX_ROPE_DIM)

    xf = x.reshape(t, d)
    for l in range(depth):
        w = w_in[l]
        o = 0
        parts = {}
        for name, size in (("c_q", DSA_Q_RANK), ("k_a", DSA_W), ("v_a", DSA_W), ("k_idx", IDX_DIM),
                           ("w_idx", IDX_HEADS), ("q_b", MOBA_W), ("k_b", MOBA_W), ("v_b", MOBA_W),
                           ("gl_a", d), ("gl_b", d)):
            parts[name] = w[:, o:o + size]
            o += size
        w_rope = jnp.concatenate([parts["k_a"], parts["q_b"], parts["k_b"]], axis=1).astype(BF16)
        w_v_t = jnp.concatenate([parts["v_a"], parts["v_b"]], axis=1).astype(BF16).T
        pad = jnp.zeros((d, LANES - IDX_DIM - IDX_HEADS), F32)
        w_small = jnp.concatenate([parts["c_q"], parts["k_idx"], parts["w_idx"], pad],
                                  axis=1).astype(BF16)
        w_gate = jnp.concatenate([parts["gl_a"], parts["gl_b"]], axis=1).astype(BF16)
        w_iq_x = jnp.pad(w_iq[l].reshape(DSA_Q_RANK, IDX_HEADS, IDX_DIM),
                         ((0, 0), (0, 0), (0, LANES - IDX_DIM)))
        w_iq_x = w_iq_x.reshape(DSA_Q_RANK, IDX_HEADS * LANES).astype(BF16)

        h = _rmsnorm_cast(xf, g_mix[l], tm=PROJ_TM, name="mix_norm")
        q_b_tiles = (MOBA_W // PROJ_TN, 2 * MOBA_W // PROJ_TN)
        p1, kmean = _project(h, w_rope, out_dtype=BF16, tm=PROJ_TM, tn=PROJ_TN,
                             name="inproj_rope", rope_tab=tab_head,
                             rope_slabs=(True,) * (PROJ_TN // LANES), rope_half=ROPE_DIM // 2,
                             colmean=True, scaled_tiles=q_b_tiles, scale=QK_SCALE)
        kmean = kmean.reshape(t // MOBA_BLOCK, w_rope.shape[1])
        p2 = _project_t(h, w_v_t, out_dtype=BF16, tm=PROJ_TM, tn=PROJ_TN, name="inproj_v")
        n_small = w_small.shape[1]
        p3 = _project(h, w_small, out_dtype=F32, tm=PROJ_TM, tn=n_small,
                      name="inproj_small", rope_tab=tab_idx,
                      rope_slabs=(False,) * (DSA_Q_RANK // LANES) + (True,),
                      rope_half=IDX_ROPE_DIM // 2)
        qx = _project(p3, w_iq_x, g=g_cq[l], out_dtype=BF16, tm=512, tn=1024, name="q_idx_upproj",
                      rope_tab=tab_idx, rope_slabs=(True,) * 8, rope_half=IDX_ROPE_DIM // 2)
        q_a = _project(p3, w_uq[l].astype(BF16), g=g_cq[l], out_dtype=BF16, tm=512, tn=1024,
                       name="q_upproj", rope_tab=tab_head, rope_slabs=(True,) * 8,
                       rope_half=ROPE_DIM // 2, scaled_tiles=(0, 1), scale=QK_SCALE)
        lane = jnp.arange(LANES)
        kx = jnp.where(lane[None, :] < IDX_DIM, p3[:, DSA_Q_RANK:], 0.0).astype(BF16)
        bias = _dsa_select(qx, p3, kx, batch=batch, seq=seq)
        o_a = _dsa_attention(q_a, p1, p2, bias, batch=batch, seq=seq)
        rb = _moba_gate(p1, kmean, batch=batch, seq=seq)
        o_b = _moba_attention(p1, p2, rb, batch=batch, seq=seq)
        merged = _gated_merge(h, o_a, o_b, w_gate, w_dsa_o[l].astype(BF16),
                              w_moba_o[l].astype(BF16), tm=PROJ_TM, tn=PROJ_TN)
        x1 = _matmul_residual(merged, w_out[l].astype(BF16), xf, tm=PROJ_TM, tn=PROJ_TN,
                              name="w_out_res")
        kv = _project(mem.reshape(batch * mem.shape[1], d), w_mem_kv[l].astype(BF16),
                      g=g_mem_kv[l], out_dtype=BF16, tm=256, tn=512, name="mem_kv")
        kv = kv.reshape(batch, mem.shape[1], 2 * MEM_W)
        x2 = _mem_attention(x1, g_mem_q[l], w_mem_q[l].astype(BF16), kv, w_mem_o[l].astype(BF16),
                            batch=batch, seq=seq)
        xf = _ffn_final(x2, g_ff[l], w_ff1[l].astype(BF16), w_ff2[l].astype(BF16), g_final)
    return xf.reshape(batch, seq, d)
```

```python
import functools

import jax
import jax.numpy as jnp
from jax import lax
from jax.experimental import pallas as pl
from jax.experimental.pallas import tpu as pltpu

F32 = jnp.float32
BF16 = jnp.bfloat16

D_MODEL = 2048
HEAD_DIM = 128
ROPE_DIM = 32
ROPE_THETA = 500000.0
DSA_HEADS = 8
DSA_W = DSA_HEADS * HEAD_DIM
DSA_Q_RANK = 512
IDX_HEADS = 16
IDX_DIM = 64
IDX_ROPE_DIM = 16
IDX_TOPK = 256
MOBA_HEADS = 8
MOBA_W = MOBA_HEADS * HEAD_DIM
MOBA_BLOCK = 256
MOBA_TOPK = 3
MEM_HEADS = 4
MEM_HEAD_DIM = 128
MEM_W = MEM_HEADS * MEM_HEAD_DIM
D_FF = 4 * D_MODEL
EPS = 1e-6

LANES = 128
SUBLANES = 8
NEG = -1e30
INT_MIN = -(2 ** 31)
VMEM_LIMIT = 48 * 1024 * 1024

LOG2E = 1.4426950408889634
QK_SCALE = HEAD_DIM ** -0.5 * LOG2E

_NT = (((1,), (1,)), ((), ()))
_TN = (((0,), (0,)), ((), ()))


def _cparams(sem, vmem_limit=VMEM_LIMIT):
    return pltpu.CompilerParams(dimension_semantics=sem, vmem_limit_bytes=vmem_limit)


PROJ_TM = 1024
PROJ_TN = 512


def _rmsnorm_kernel(x_ref, g_ref, o_ref):
    x = x_ref[...]
    ms = jnp.mean(x * x, axis=-1, keepdims=True)
    o_ref[...] = (x * lax.rsqrt(ms + EPS) * g_ref[...]).astype(o_ref.dtype)


def _rmsnorm_cast(x, g, *, tm, name):
    m, k = x.shape
    return pl.pallas_call(
        _rmsnorm_kernel,
        grid=(m // tm,),
        in_specs=[pl.BlockSpec((tm, k), lambda i: (i, 0)), pl.BlockSpec((1, k), lambda i: (0, 0))],
        out_specs=pl.BlockSpec((tm, k), lambda i: (i, 0)),
        out_shape=jax.ShapeDtypeStruct((m, k), BF16),
        compiler_params=_cparams(("parallel",)),
        name=name,
    )(x, g.reshape(1, k))


def _proj_t_kernel(a_ref, wt_ref, o_ref):
    o_ref[...] = lax.dot_general(wt_ref[...], a_ref[...], _NT,
                                 preferred_element_type=F32).astype(o_ref.dtype)


def _project_t(a, wt, *, out_dtype, tm, tn, name):
    m, k = a.shape
    n = wt.shape[0]
    return pl.pallas_call(
        _proj_t_kernel,
        grid=(m // tm, n // tn),
        in_specs=[
            pl.BlockSpec((tm, k), lambda i, j: (i, 0)),
            pl.BlockSpec((tn, k), lambda i, j: (j, 0)),
        ],
        out_specs=pl.BlockSpec((tn, tm), lambda i, j: (j, i)),
        out_shape=jax.ShapeDtypeStruct((n, m), out_dtype),
        compiler_params=_cparams(("parallel", "parallel")),
        name=name,
    )(a, wt)


def _proj_kernel(*refs, norm, rope_slabs, rope_half, n_mean, has_tab, scaled_tiles, scale):
    refs = list(refs)
    x_ref = refs.pop(0)
    g_ref = refs.pop(0) if norm else None
    w_ref = refs.pop(0)
    tab_ref = refs.pop(0) if has_tab else None
    o_ref = refs.pop(0)
    mean_ref = refs.pop(0) if n_mean else None
    h_scr = refs.pop(0) if norm else None
    j = pl.program_id(1)

    if norm:
        @pl.when(j == 0)
        def _():
            x = x_ref[...]
            ms = jnp.mean(x * x, axis=-1, keepdims=True)
            h_scr[...] = (x * lax.rsqrt(ms + EPS) * g_ref[...]).astype(BF16)
        lhs = h_scr[...]
    else:
        lhs = x_ref[...]

    acc = jnp.dot(lhs, w_ref[...], preferred_element_type=F32)
    tm, tn = acc.shape
    if scaled_tiles is not None:
        lo, hi = scaled_tiles
        post = jnp.where(jnp.logical_and(j >= lo, j < hi), scale, 1.0)
    for c in range(tn // LANES):
        y = acc[:, c * LANES:(c + 1) * LANES]
        if rope_slabs is not None and rope_slabs[c]:
            cs = tab_ref[:, 0:LANES]
            sm = tab_ref[:, LANES:2 * LANES]
            sp = tab_ref[:, 2 * LANES:3 * LANES]
            y = (y * cs + pltpu.roll(y, LANES - rope_half, 1) * sm
                 + pltpu.roll(y, rope_half, 1) * sp)
        if n_mean:
            for r in range(n_mean):
                blk = y[r * MOBA_BLOCK:(r + 1) * MOBA_BLOCK, :]
                mean_ref[r, :, c * LANES:(c + 1) * LANES] = jnp.mean(blk, axis=0, keepdims=True)
        if scaled_tiles is not None:
            y = y * post
        o_ref[:, c * LANES:(c + 1) * LANES] = y.astype(o_ref.dtype)


def _project(x, w, *, out_dtype, tm, tn, name, g=None, rope_tab=None, rope_slabs=None,
             rope_half=0, colmean=False, scaled_tiles=None, scale=1.0):
    m = x.shape[0]
    k, n = w.shape
    norm = g is not None
    assert m % tm == 0 and n % tn == 0 and tn % LANES == 0
    n_mean = tm // MOBA_BLOCK if colmean else 0
    in_specs = [pl.BlockSpec((tm, k), lambda i, j: (i, 0))]
    args = [x]
    if norm:
        in_specs.append(pl.BlockSpec((1, k), lambda i, j: (0, 0)))
        args.append(g.reshape(1, k))
    in_specs.append(pl.BlockSpec((k, tn), lambda i, j: (0, j)))
    args.append(w)
    if rope_tab is not None:
        in_specs.append(pl.BlockSpec((tm, 3 * LANES), lambda i, j: (i, 0)))
        args.append(rope_tab)
    out_shape = [jax.ShapeDtypeStruct((m, n), out_dtype)]
    out_specs = [pl.BlockSpec((tm, tn), lambda i, j: (i, j))]
    if colmean:
        out_shape.append(jax.ShapeDtypeStruct((m // MOBA_BLOCK, 1, n), F32))
        out_specs.append(pl.BlockSpec((n_mean, 1, tn), lambda i, j: (i, 0, j)))
    kern = functools.partial(_proj_kernel, norm=norm, rope_slabs=rope_slabs, rope_half=rope_half,
                             n_mean=n_mean, has_tab=rope_tab is not None,
                             scaled_tiles=scaled_tiles, scale=scale)
    res = pl.pallas_call(
        kern,
        grid=(m // tm, n // tn),
        in_specs=in_specs,
        out_specs=out_specs,
        out_shape=out_shape,
        scratch_shapes=[pltpu.VMEM((tm, k), BF16)] if norm else [],
        compiler_params=_cparams(("parallel", "arbitrary" if norm else "parallel")),
        name=name,
    )(*args)
    return res if colmean else res[0]


def _mm_res_kernel(a_ref, w_ref, r_ref, o_ref):
    o_ref[...] = r_ref[...] + jnp.dot(a_ref[...], w_ref[...], preferred_element_type=F32)


def _matmul_residual(a, w, res, *, tm, tn, name):
    m, k = a.shape
    n = w.shape[1]
    return pl.pallas_call(
        _mm_res_kernel,
        grid=(m // tm, n // tn),
        in_specs=[
            pl.BlockSpec((tm, k), lambda i, j: (i, 0)),
            pl.BlockSpec((k, tn), lambda i, j: (0, j)),
            pl.BlockSpec((tm, tn), lambda i, j: (i, j)),
        ],
        out_specs=pl.BlockSpec((tm, tn), lambda i, j: (i, j)),
        out_shape=jax.ShapeDtypeStruct((m, n), F32),
        compiler_params=_cparams(("parallel", "parallel")),
        name=name,
    )(a, w, res)


IDX_TQ = 256
IDX_KC = 512
BISECT_UNROLL = 2
COUNT_WAYS = 4


def _ordered_bits(v):
    return v ^ ((v >> 31) & 0x7FFFFFFF)


def _indexer_kernel(qx_ref, w_ref, kx_ref, bias_ref, sc_scr, g_scr, *, seq):
    i = pl.program_id(1)
    tq, kc_len = IDX_TQ, IDX_KC
    q0 = i * tq
    n_chunks = (q0 + tq + kc_len - 1) // kc_len
    total_chunks = seq // kc_len

    w_t = (w_ref[...] * (IDX_HEADS ** -0.5 * IDX_DIM ** -0.5)).T
    q_pos = q0 + lax.broadcasted_iota(jnp.int32, (kc_len, tq), 1)
    k_off = lax.broadcasted_iota(jnp.int32, (kc_len, tq), 0)

    def score_chunk(kc, carry):
        k0 = pl.multiple_of(kc * kc_len, kc_len)
        rows = pl.ds(k0, kc_len)
        kx = kx_ref[rows, :]
        for h in range(IDX_HEADS):
            qh = qx_ref[:, h * LANES:(h + 1) * LANES]
            logit = lax.dot_general(kx, qh, _NT, preferred_element_type=F32)
            term = jnp.maximum(logit, 0.0) * w_t[IDX_DIM + h:IDX_DIM + h + 1, :]
            if h == 0:
                sc_scr[rows, :] = term
            elif h < IDX_HEADS - 1:
                sc_scr[rows, :] += term
            else:
                val = jnp.where(k0 + k_off <= q_pos, sc_scr[rows, :] + term, -jnp.inf)
                sc_scr[rows, :] = val
                fold = jnp.max(val.reshape(kc_len // IDX_TOPK, IDX_TOPK, tq), axis=0)
                g_scr[...] = jnp.maximum(g_scr[...], fold)
        return carry

    g_scr[...] = jnp.full(g_scr.shape, -jnp.inf, F32)
    lax.fori_loop(0, n_chunks, score_chunk, 0)

    def count(pred_fn):
        def body(kc, cnt):
            k0 = pl.multiple_of(kc * kc_len, kc_len)
            blk = sc_scr[pl.ds(k0, kc_len), :]
            hit = jnp.where(pred_fn(blk, k0 + k_off), 1.0, 0.0)
            hit = hit.reshape(kc_len // (SUBLANES * COUNT_WAYS), COUNT_WAYS, SUBLANES, tq)
            part = hit[0]
            for r in range(1, hit.shape[0]):
                part = part + hit[r]
            return cnt + part
        cnt = lax.fori_loop(0, n_chunks, body, jnp.zeros((COUNT_WAYS, SUBLANES, tq), F32))
        return jnp.sum(jnp.sum(cnt, axis=0), axis=0, keepdims=True)

    g = g_scr[...]
    lo_f = jnp.min(g, axis=0, keepdims=True)
    hi_f = jnp.max(g, axis=0, keepdims=True)
    has_thr = lo_f > -jnp.inf

    def code(v):
        return _ordered_bits(pltpu.bitcast(v, jnp.int32)) ^ INT_MIN

    def value(c):
        return pltpu.bitcast(_ordered_bits(c ^ INT_MIN), F32)

    lo0 = jnp.where(has_thr, code(lo_f), 0)
    hi0 = jnp.where(has_thr, code(hi_f) + 1, 0)

    def open_width(lo, hi):
        d = hi - lo
        return jnp.where(jnp.logical_and(d != 0, d != 1), 1.0, 0.0)

    def bisect(lo, hi):
        mid = lo + lax.shift_right_logical(hi - lo, 1)
        active = open_width(lo, hi) > 0.0
        trial = value(mid)
        n_ge = count(lambda blk, _: blk >= trial)
        up = jnp.logical_and(active, n_ge >= float(IDX_TOPK))
        down = jnp.logical_and(active, n_ge <= float(IDX_TOPK))
        return jnp.where(up, mid, lo), jnp.where(down, mid, hi)

    def search_body(carry):
        lo, hi, _ = carry
        for _ in range(BISECT_UNROLL):
            lo, hi = bisect(lo, hi)
        return lo, hi, jnp.max(open_width(lo, hi))

    lo_c, _, _ = lax.while_loop(lambda c: c[2] > 0.0, search_body,
                                (lo0, hi0, jnp.max(open_width(lo0, hi0))))
    thr = jnp.where(has_thr, value(lo_c), -jnp.inf)

    n_gt = count(lambda blk, _: blk > thr)
    n_eq = count(lambda blk, _: blk == thr)
    need = float(IDX_TOPK) - n_gt
    excess = jnp.logical_and(has_thr, n_eq > need)
    idx_all = jnp.where(has_thr, jnp.int32(seq), jnp.int32(-1))
    any_excess = jnp.max(jnp.where(excess, 1.0, 0.0)) > 0.0

    def tie_search():
        def jb(b, lo):
            trial = lo + lax.shift_left(jnp.int32(1), 12 - b)
            n = count(lambda blk, kidx: jnp.logical_and(blk == thr, kidx <= trial - 1))
            return jnp.where(n < need, trial, lo)
        lo = lax.fori_loop(0, 13, jb, jnp.zeros((1, tq), jnp.int32))
        return jnp.where(excess, lo, idx_all)

    thr_idx = lax.cond(any_excess, tie_search, lambda: idx_all)

    def write_chunk(kc, carry):
        k0 = pl.multiple_of(kc * kc_len, kc_len)
        blk = sc_scr[pl.ds(k0, kc_len), :]
        sel = jnp.logical_or(blk > thr, jnp.logical_and(blk == thr, k0 + k_off <= thr_idx))
        bias_ref[0, pl.ds(k0, kc_len), :] = jnp.where(sel, 0.0, NEG).astype(BF16)
        return carry

    lax.fori_loop(0, n_chunks, write_chunk, 0)

    def fill_chunk(kc, carry):
        k0 = pl.multiple_of(kc * kc_len, kc_len)
        bias_ref[0, pl.ds(k0, kc_len), :] = jnp.full((kc_len, tq), NEG, BF16)
        return carry

    lax.fori_loop(n_chunks, total_chunks, fill_chunk, 0)


def _dsa_select(qx, p3, kx, *, batch, seq):
    nq = seq // IDX_TQ
    return pl.pallas_call(
        functools.partial(_indexer_kernel, seq=seq),
        grid=(batch, nq),
        in_specs=[
            pl.BlockSpec((IDX_TQ, IDX_HEADS * LANES), lambda b, i: (b * nq + i, 0)),
            pl.BlockSpec((IDX_TQ, LANES), lambda b, i: (b * nq + i, DSA_Q_RANK // LANES)),
            pl.BlockSpec((seq, LANES), lambda b, i: (b, 0)),
        ],
        out_specs=pl.BlockSpec((1, seq, IDX_TQ), lambda b, i: (b, 0, i)),
        out_shape=jax.ShapeDtypeStruct((batch, seq, seq), BF16),
        scratch_shapes=[pltpu.VMEM((seq, IDX_TQ), F32), pltpu.VMEM((IDX_TOPK, IDX_TQ), F32)],
        compiler_params=_cparams(("parallel", "parallel")),
        name="dsa_indexer_select",
    )(qx, p3, kx)


FLASH_CHUNK = MOBA_BLOCK


def _flash_update(q_ref, k_ref, vt_ref, rows, bias_fn, st, n_heads):
    m_scr, l_scr, a_scr, acc_scr, s_scr = st[:5]
    chunks = [slice(r, r + FLASH_CHUNK) for r in range(0, rows.stop - rows.start, FLASH_CHUNK)]
    for h in range(n_heads):
        sl = slice(h * HEAD_DIM, (h + 1) * HEAD_DIM)
        m_prev = m_scr[h]
        m_new = m_prev
        for c, ch in enumerate(chunks):
            krows = slice(rows.start + ch.start, rows.start + ch.stop)
            s = lax.dot_general(k_ref[krows, sl], q_ref[:, sl], _NT, preferred_element_type=F32)
            s = s + bias_fn(h, c)
            s_scr[h, ch, :] = s
            m_new = jnp.maximum(m_new, jnp.max(s, axis=0, keepdims=True))
        a_scr[h] = jnp.exp2(m_prev - m_new)
        m_scr[h] = m_new
    for h in range(n_heads):
        sl = slice(h * HEAD_DIM, (h + 1) * HEAD_DIM)
        alpha = a_scr[h]
        m_new = m_scr[h]
        l_new = alpha * l_scr[h]
        acc = alpha * acc_scr[sl, :]
        for ch in chunks:
            krows = slice(rows.start + ch.start, rows.start + ch.stop)
            p = jnp.exp2(s_scr[h, ch, :] - m_new)
            l_new = l_new + jnp.sum(p, axis=0, keepdims=True)
            acc = acc + jnp.dot(vt_ref[sl, krows], p.astype(BF16), preferred_element_type=F32)
        l_scr[h] = l_new
        acc_scr[sl, :] = acc


LAZY_MAX_GROWTH = 60.0


def _flash_step(q_ref, k_ref, vt_ref, rows, bias_fn, st, n_heads, first=None):
    m_scr, l_scr, a_scr, acc_scr, s_scr, x_scr, p_scr = st
    chunks = [slice(r, r + FLASH_CHUNK) for r in range(0, rows.stop - rows.start, FLASH_CHUNK)]

    def tentative():
        bad = jnp.zeros(m_scr.shape[1:], F32)
        for h in range(n_heads):
            sl = slice(h * HEAD_DIM, (h + 1) * HEAD_DIM)
            m_old = m_scr[h]
            mx = m_old
            l_add = jnp.zeros_like(m_old)
            for c, ch in enumerate(chunks):
                krows = slice(rows.start + ch.start, rows.start + ch.stop)
                s = lax.dot_general(k_ref[krows, sl], q_ref[:, sl], _NT,
                                    preferred_element_type=F32)
                s = s + bias_fn(h, c)
                mx = jnp.maximum(mx, jnp.max(s, axis=0, keepdims=True))
                p = jnp.exp2(s - m_old)
                l_add = l_add + jnp.sum(p, axis=0, keepdims=True)
                p_scr[h, ch, :] = p.astype(BF16)
            a_scr[h] = l_add
            x_scr[h] = mx
            bad = jnp.maximum(bad, jnp.where(mx - m_old > LAZY_MAX_GROWTH, 1.0, 0.0))
        return jnp.max(bad)

    if first is None:
        unsafe = tentative()
    else:
        unsafe = lax.cond(first, lambda: jnp.float32(1.0), tentative)

    @pl.when(unsafe == 0.0)
    def _():
        for h in range(n_heads):
            sl = slice(h * HEAD_DIM, (h + 1) * HEAD_DIM)
            m_new = x_scr[h]
            r = jnp.exp2(m_scr[h] - m_new)
            pv = None
            for ch in chunks:
                krows = slice(rows.start + ch.start, rows.start + ch.stop)
                d = jnp.dot(vt_ref[sl, krows], p_scr[h, ch, :], preferred_element_type=F32)
                pv = d if pv is None else pv + d
            l_scr[h] = (l_scr[h] + a_scr[h]) * r
            acc_scr[sl, :] = (acc_scr[sl, :] + pv) * r
            m_scr[h] = m_new

    @pl.when(unsafe != 0.0)
    def _():
        _flash_update(q_ref, k_ref, vt_ref, rows, bias_fn, st, n_heads)


def _flash_init(st):
    m_scr, l_scr, _, acc_scr = st[:4]
    m_scr[...] = jnp.full(m_scr.shape, -jnp.inf, F32)
    l_scr[...] = jnp.zeros(l_scr.shape, F32)
    acc_scr[...] = jnp.zeros(acc_scr.shape, F32)


def _flash_finish(o_ref, st, n_heads):
    _, l_scr, _, acc_scr = st[:4]
    for h in range(n_heads):
        sl = slice(h * HEAD_DIM, (h + 1) * HEAD_DIM)
        o = acc_scr[sl, :] / l_scr[h]
        o_ref[:, sl] = o.T.astype(o_ref.dtype)


def _flash_scratch(n_heads, tq, tk):
    row = pltpu.VMEM((n_heads, 1, tq), F32)
    return [row, row, row, pltpu.VMEM((n_heads * HEAD_DIM, tq), F32),
            pltpu.VMEM((n_heads, tk, tq), F32), row, pltpu.VMEM((n_heads, tk, tq), BF16)]


DSA_TQ = 256
DSA_TK = 512


def _causal_pairs(nq, tq, tk):
    qi, kt = [], []
    for i in range(nq):
        for k in range((i * tq + tq - 1) // tk + 1):
            qi.append(i)
            kt.append(k)
    return jnp.asarray(qi, jnp.int32), jnp.asarray(kt, jnp.int32)


def _dsa_attn_kernel(qi_ref, kt_ref, q_ref, k_ref, vt_ref, b_ref, o_ref, *st):
    s = pl.program_id(1)
    i = qi_ref[s]
    kt = kt_ref[s]
    last = (i * DSA_TQ + DSA_TQ - 1) // DSA_TK

    @pl.when(kt == 0)
    def _():
        _flash_init(st)

    bias = b_ref[0].astype(F32)
    _flash_step(q_ref, k_ref, vt_ref, slice(0, DSA_TK),
                lambda h, c: bias[c * FLASH_CHUNK:(c + 1) * FLASH_CHUNK], st, DSA_HEADS,
                first=kt == 0)

    @pl.when(kt == last)
    def _():
        _flash_finish(o_ref, st, DSA_HEADS)


def _dsa_attention(q, p1, p2, bias, *, batch, seq):
    nq = seq // DSA_TQ
    nk = seq // DSA_TK
    qi, kt = _causal_pairs(nq, DSA_TQ, DSA_TK)
    grid_spec = pltpu.PrefetchScalarGridSpec(
        num_scalar_prefetch=2,
        grid=(batch, qi.shape[0]),
        in_specs=[
            pl.BlockSpec((DSA_TQ, DSA_W), lambda b, s, qi, kt: (b * nq + qi[s], 0)),
            pl.BlockSpec((DSA_TK, DSA_W), lambda b, s, qi, kt: (b * nk + kt[s], 0)),
            pl.BlockSpec((DSA_W, DSA_TK), lambda b, s, qi, kt: (0, b * nk + kt[s])),
            pl.BlockSpec((1, DSA_TK, DSA_TQ), lambda b, s, qi, kt: (b, kt[s], qi[s])),
        ],
        out_specs=pl.BlockSpec((DSA_TQ, DSA_W), lambda b, s, qi, kt: (b * nq + qi[s], 0)),
        scratch_shapes=_flash_scratch(DSA_HEADS, DSA_TQ, DSA_TK),
    )
    return pl.pallas_call(
        _dsa_attn_kernel,
        grid_spec=grid_spec,
        out_shape=jax.ShapeDtypeStruct((batch * seq, DSA_W), BF16),
        compiler_params=_cparams(("parallel", "arbitrary")),
        name="dsa_attention",
    )(qi, kt, q, p1, p2, bias)


def _moba_gate_kernel(q_ref, km_ref, rb_ref, *, nb):
    i = pl.program_id(1)
    tq = q_ref.shape[0]
    jidx = lax.broadcasted_iota(jnp.int32, (nb, tq), 0)
    for h in range(MOBA_HEADS):
        sl = slice(h * HEAD_DIM, (h + 1) * HEAD_DIM)
        km = km_ref[:, sl].astype(BF16)
        g = lax.dot_general(km, q_ref[:, sl], _NT, preferred_element_type=F32)
        g = jnp.where(jidx < i, g, -jnp.inf)
        rank = jnp.zeros((nb, tq), F32)
        for jp in range(nb):
            row = g[jp:jp + 1, :]
            tie_ahead = jnp.where(jidx > jp, 1.0, 0.0)
            rank = rank + jnp.where(row > g, 1.0, jnp.where(row == g, tie_ahead, 0.0))
        sel = jnp.logical_and(rank < float(MOBA_TOPK), jidx < i)
        rb_ref[0, h] = jnp.where(sel, 0.0, NEG)


def _moba_gate(p1, kmean, *, batch, seq):
    nb = seq // MOBA_BLOCK
    return pl.pallas_call(
        functools.partial(_moba_gate_kernel, nb=nb),
        grid=(batch, nb),
        in_specs=[
            pl.BlockSpec((MOBA_BLOCK, MOBA_W), lambda b, i: (b * nb + i, 1)),
            pl.BlockSpec((nb, MOBA_W), lambda b, i: (b, 2)),
        ],
        out_specs=pl.BlockSpec((1, MOBA_HEADS, nb, MOBA_BLOCK), lambda b, i: (b, 0, 0, i)),
        out_shape=jax.ShapeDtypeStruct((batch, MOBA_HEADS, nb, seq), F32),
        compiler_params=_cparams(("parallel", "parallel")),
        name="moba_gate",
    )(p1, kmean)


MOBA_TQ = MOBA_BLOCK
MOBA_TK = 2 * MOBA_BLOCK


def _moba_pairs(nq):
    qi, kt = [], []
    for i in range(nq):
        for k in range(max(1, (i + 1) // 2)):
            qi.append(i)
            kt.append(k)
    return jnp.asarray(qi, jnp.int32), jnp.asarray(kt, jnp.int32)


def _moba_attn_kernel(qi_ref, kt_ref, q_ref, kown_ref, vtown_ref, k_ref, vt_ref, rb_ref, o_ref,
                      *st):
    s = pl.program_id(1)
    i = qi_ref[s]
    kt = kt_ref[s]
    n_full = i // 2
    last_step = jnp.maximum((i + 1) // 2, 1) - 1
    j0 = 2 * kt

    def row_mask(h, j):
        return rb_ref[0, h, pl.ds(j, 1), :]

    @pl.when(kt == 0)
    def _():
        _flash_init(st)
        key = lax.broadcasted_iota(jnp.int32, (MOBA_BLOCK, MOBA_TQ), 0)
        qry = lax.broadcasted_iota(jnp.int32, (MOBA_BLOCK, MOBA_TQ), 1)
        causal = jnp.where(key <= qry, 0.0, NEG)
        _flash_update(q_ref, kown_ref, vtown_ref, slice(0, MOBA_BLOCK), lambda h, c: causal, st,
                      MOBA_HEADS)

    @pl.when(kt < n_full)
    def _():
        _flash_step(q_ref, k_ref, vt_ref, slice(0, MOBA_TK),
                    lambda h, c: row_mask(h, j0 + c), st, MOBA_HEADS)

    @pl.when(jnp.logical_and(kt == n_full, i % 2 == 1))
    def _():
        _flash_step(q_ref, k_ref, vt_ref, slice(0, MOBA_BLOCK),
                    lambda h, c: row_mask(h, j0), st, MOBA_HEADS)

    @pl.when(kt == last_step)
    def _():
        _flash_finish(o_ref, st, MOBA_HEADS)


def _moba_attention(p1, p2, rb, *, batch, seq):
    nq = seq // MOBA_TQ
    nk = seq // MOBA_TK
    nb = seq // MOBA_BLOCK
    qi, kt = _moba_pairs(nq)
    grid_spec = pltpu.PrefetchScalarGridSpec(
        num_scalar_prefetch=2,
        grid=(batch, qi.shape[0]),
        in_specs=[
            pl.BlockSpec((MOBA_TQ, MOBA_W), lambda b, s, qi, kt: (b * nq + qi[s], 1)),
            pl.BlockSpec((MOBA_BLOCK, MOBA_W), lambda b, s, qi, kt: (b * nb + qi[s], 2)),
            pl.BlockSpec((MOBA_W, MOBA_BLOCK), lambda b, s, qi, kt: (1, b * nb + qi[s])),
            pl.BlockSpec((MOBA_TK, MOBA_W), lambda b, s, qi, kt: (b * nk + kt[s], 2)),
            pl.BlockSpec((MOBA_W, MOBA_TK), lambda b, s, qi, kt: (1, b * nk + kt[s])),
            pl.BlockSpec((1, MOBA_HEADS, nb, MOBA_TQ), lambda b, s, qi, kt: (b, 0, 0, qi[s])),
        ],
        out_specs=pl.BlockSpec((MOBA_TQ, MOBA_W), lambda b, s, qi, kt: (b * nq + qi[s], 0)),
        scratch_shapes=_flash_scratch(MOBA_HEADS, MOBA_TQ, MOBA_TK),
    )
    return pl.pallas_call(
        _moba_attn_kernel,
        grid_spec=grid_spec,
        out_shape=jax.ShapeDtypeStruct((batch * seq, MOBA_W), BF16),
        compiler_params=_cparams(("parallel", "arbitrary")),
        name="moba_attention",
    )(qi, kt, p1, p1, p2, p1, p2, rb)


def _gated_merge_kernel(h_ref, oa_ref, ob_ref, wga_ref, wgb_ref, wa_ref, wb_ref, o_ref):
    h = h_ref[...]
    ga = jnp.dot(h, wga_ref[...], preferred_element_type=F32)
    ya = jnp.dot(oa_ref[...], wa_ref[...], preferred_element_type=F32)
    out = ya / (1.0 + jnp.exp(-ga))
    gb = jnp.dot(h, wgb_ref[...], preferred_element_type=F32)
    yb = jnp.dot(ob_ref[...], wb_ref[...], preferred_element_type=F32)
    o_ref[...] = (out + yb / (1.0 + jnp.exp(-gb))).astype(o_ref.dtype)


def _gated_merge(h, oa, ob, w_gate, wa, wb, *, tm, tn):
    m, d = h.shape
    n = wa.shape[1]
    nj = n // tn
    return pl.pallas_call(
        _gated_merge_kernel,
        grid=(m // tm, nj),
        in_specs=[
            pl.BlockSpec((tm, d), lambda i, j: (i, 0)),
            pl.BlockSpec((tm, DSA_W), lambda i, j: (i, 0)),
            pl.BlockSpec((tm, MOBA_W), lambda i, j: (i, 0)),
            pl.BlockSpec((d, tn), lambda i, j: (0, j)),
            pl.BlockSpec((d, tn), lambda i, j: (0, nj + j)),
            pl.BlockSpec((DSA_W, tn), lambda i, j: (0, j)),
            pl.BlockSpec((MOBA_W, tn), lambda i, j: (0, j)),
        ],
        out_specs=pl.BlockSpec((tm, tn), lambda i, j: (i, j)),
        out_shape=jax.ShapeDtypeStruct((m, n), BF16),
        compiler_params=_cparams(("parallel", "parallel")),
        name="gated_merge",
    )(h, oa, ob, w_gate, w_gate, wa, wb)


MEM_TM = 256


def _mem_attn_kernel(x_ref, g_ref, wq_ref, kv_ref, wo_ref, o_ref):
    x = x_ref[...]
    ms = jnp.mean(x * x, axis=-1, keepdims=True)
    hm = (x * lax.rsqrt(ms + EPS) * g_ref[...]).astype(BF16)
    q = jnp.dot(hm, wq_ref[...], preferred_element_type=F32).astype(BF16)
    scale = MEM_HEAD_DIM ** -0.5
    outs = []
    for h in range(MEM_HEADS):
        sl = slice(h * MEM_HEAD_DIM, (h + 1) * MEM_HEAD_DIM)
        kh = kv_ref[0, :, sl]
        vh = kv_ref[0, :, MEM_W + h * MEM_HEAD_DIM:MEM_W + (h + 1) * MEM_HEAD_DIM]
        s = lax.dot_general(q[:, sl], kh, _NT, preferred_element_type=F32) * scale
        e = jnp.exp(s - jnp.max(s, axis=1, keepdims=True))
        p = e / jnp.sum(e, axis=1, keepdims=True)
        outs.append(jnp.dot(p.astype(BF16), vh, preferred_element_type=F32).astype(BF16))
    o = jnp.concatenate(outs, axis=1)
    o_ref[...] = x + jnp.dot(o, wo_ref[...], preferred_element_type=F32)


def _mem_attention(x1, g, wq, kv, wo, *, batch, seq):
    t, d = x1.shape
    mem_len = kv.shape[1]
    per_b = seq // MEM_TM
    return pl.pallas_call(
        _mem_attn_kernel,
        grid=(t // MEM_TM,),
        in_specs=[
            pl.BlockSpec((MEM_TM, d), lambda i: (i, 0)),
            pl.BlockSpec((1, d), lambda i: (0, 0)),
            pl.BlockSpec((d, MEM_W), lambda i: (0, 0)),
            pl.BlockSpec((1, mem_len, 2 * MEM_W), lambda i: (i // per_b, 0, 0)),
            pl.BlockSpec((MEM_W, d), lambda i: (0, 0)),
        ],
        out_specs=pl.BlockSpec((MEM_TM, d), lambda i: (i, 0)),
        out_shape=jax.ShapeDtypeStruct((t, d), F32),
        compiler_params=_cparams(("parallel",)),
        name="mem_cross_attention",
    )(x1, g.reshape(1, d), wq, kv, wo)


FFN_TM = 1024
FFN_TF = 512
FFN_VMEM_LIMIT = 54 * 1024 * 1024


def _ffn_kernel(x_ref, g_ref, w1_ref, w2_ref, gf_ref, o_ref, h_scr):
    f = pl.program_id(1)

    @pl.when(f == 0)
    def _():
        x = x_ref[...]
        ms = jnp.mean(x * x, axis=-1, keepdims=True)
        h_scr[...] = (x * lax.rsqrt(ms + EPS) * g_ref[...]).astype(BF16)
        o_ref[...] = x

    u = jnp.dot(h_scr[...], w1_ref[...], preferred_element_type=F32)
    u = jnp.maximum(u, 0.0)
    u = (u * u).astype(BF16)
    o_ref[...] += jnp.dot(u, w2_ref[...], preferred_element_type=F32)

    @pl.when(f == pl.num_programs(1) - 1)
    def _():
        y = o_ref[...]
        ms = jnp.mean(y * y, axis=-1, keepdims=True)
        o_ref[...] = y * lax.rsqrt(ms + EPS) * gf_ref[...]


def _ffn_final(x2, g_ff, w1, w2, g_final):
    t, d = x2.shape
    dff = w1.shape[1]
    return pl.pallas_call(
        _ffn_kernel,
        grid=(t // FFN_TM, dff // FFN_TF),
        in_specs=[
            pl.BlockSpec((FFN_TM, d), lambda i, f: (i, 0)),
            pl.BlockSpec((1, d), lambda i, f: (0, 0)),
            pl.BlockSpec((d, FFN_TF), lambda i, f: (0, f)),
            pl.BlockSpec((FFN_TF, d), lambda i, f: (f, 0)),
            pl.BlockSpec((1, d), lambda i, f: (0, 0)),
        ],
        out_specs=pl.BlockSpec((FFN_TM, d), lambda i, f: (i, 0)),
        out_shape=jax.ShapeDtypeStruct((t, d), F32),
        scratch_shapes=[pltpu.VMEM((FFN_TM, d), BF16)],
        compiler_params=_cparams(("parallel", "arbitrary"), vmem_limit=FFN_VMEM_LIMIT),
        name="ffn_final_norm",
    )(x2, g_ff.reshape(1, d), w1, w2, g_final.reshape(1, d))


def _rope_table(cos, sin):
    t, half = cos.shape
    rot_dim = 2 * half
    ones = jnp.ones((t, LANES - rot_dim), F32)
    zeros_h = jnp.zeros((t, half), F32)
    zeros_r = jnp.zeros((t, LANES - rot_dim), F32)
    c = jnp.concatenate([cos, cos, ones], axis=1)
    sm = jnp.concatenate([-sin, zeros_h, zeros_r], axis=1)
    sp = jnp.concatenate([zeros_h, sin, zeros_r], axis=1)
    return jnp.concatenate([c, sm, sp], axis=1)


def kernel(x, mem, positions, g_mix, w_in, g_cq, w_uq, w_iq, w_dsa_o, w_moba_o, w_out,
           g_mem_q, g_mem_kv, w_mem_q, w_mem_kv, w_mem_o, g_ff, w_ff1, w_ff2, g_final):
    batch, seq, d = x.shape
    t = batch * seq
    depth = w_in.shape[0]
    assert depth == 1 and seq % (2 * MOBA_BLOCK) == 0 and d == D_MODEL
    pos = positions.reshape(t)
    def rope_cos_sin(rot_dim):
        inv = ROPE_THETA ** (-jnp.arange(rot_dim // 2, dtype=F32) * (2.0 / rot_dim))
        ang = pos.astype(F32)[:, None] * inv
        return jnp.cos(ang), jnp.sin(ang)

    tab_head = _rope_table(*rope_cos_sin(ROPE_DIM))
    tab_idx = _rope_table(*rope_cos_sin(IDX_ROPE_DIM))

    xf = x.reshape(t, d)
    for l in range(depth):
        w = w_in[l]
        o = 0
        parts = {}
        for name, size in (("c_q", DSA_Q_RANK), ("k_a", DSA_W), ("v_a", DSA_W), ("k_idx", IDX_DIM),
                           ("w_idx", IDX_HEADS), ("q_b", MOBA_W), ("k_b", MOBA_W), ("v_b", MOBA_W),
                           ("gl_a", d), ("gl_b", d)):
            parts[name] = w[:, o:o + size]
            o += size
        w_rope = jnp.concatenate([parts["k_a"], parts["q_b"], parts["k_b"]], axis=1).astype(BF16)
        w_v_t = jnp.concatenate([parts["v_a"], parts["v_b"]], axis=1).astype(BF16).T
        pad = jnp.zeros((d, LANES - IDX_DIM - IDX_HEADS), F32)
        w_small = jnp.concatenate([parts["c_q"], parts["k_idx"], parts["w_idx"], pad],
                                  axis=1).astype(BF16)
        w_gate = jnp.concatenate([parts["gl_a"], parts["gl_b"]], axis=1).astype(BF16)
        w_iq_x = jnp.pad(w_iq[l].reshape(DSA_Q_RANK, IDX_HEADS, IDX_DIM),
                         ((0, 0), (0, 0), (0, LANES - IDX_DIM)))
        w_iq_x = w_iq_x.reshape(DSA_Q_RANK, IDX_HEADS * LANES).astype(BF16)

        h = _rmsnorm_cast(xf, g_mix[l], tm=PROJ_TM, name="mix_norm")
        q_b_tiles = (MOBA_W // PROJ_TN, 2 * MOBA_W // PROJ_TN)
        p1, kmean = _project(h, w_rope, out_dtype=BF16, tm=PROJ_TM, tn=PROJ_TN,
                             name="inproj_rope", rope_tab=tab_head,
                             rope_slabs=(True,) * (PROJ_TN // LANES), rope_half=ROPE_DIM // 2,
                             colmean=True, scaled_tiles=q_b_tiles, scale=QK_SCALE)
        kmean = kmean.reshape(t // MOBA_BLOCK, w_rope.shape[1])
        p2 = _project_t(h, w_v_t, out_dtype=BF16, tm=PROJ_TM, tn=PROJ_TN, name="inproj_v")
        n_small = w_small.shape[1]
        p3 = _project(h, w_small, out_dtype=F32, tm=PROJ_TM, tn=n_small,
                      name="inproj_small", rope_tab=tab_idx,
                      rope_slabs=(False,) * (DSA_Q_RANK // LANES) + (True,),
                      rope_half=IDX_ROPE_DIM // 2)
        qx = _project(p3, w_iq_x, g=g_cq[l], out_dtype=BF16, tm=512, tn=1024, name="q_idx_upproj",
                      rope_tab=tab_idx, rope_slabs=(True,) * 8, rope_half=IDX_ROPE_DIM // 2)
        q_a = _project(p3, w_uq[l].astype(BF16), g=g_cq[l], out_dtype=BF16, tm=512, tn=1024,
                       name="q_upproj", rope_tab=tab_head, rope_slabs=(True,) * 8,
                       rope_half=ROPE_DIM // 2, scaled_tiles=(0, 1), scale=QK_SCALE)
        lane = jnp.arange(LANES)
        kx = jnp.where(lane[None, :] < IDX_DIM, p3[:, DSA_Q_RANK:], 0.0).astype(BF16)
        bias = _dsa_select(qx, p3, kx, batch=batch, seq=seq)
        o_a = _dsa_attention(q_a, p1, p2, bias, batch=batch, seq=seq)
        rb = _moba_gate(p1, kmean, batch=batch, seq=seq)
        o_b = _moba_attention(p1, p2, rb, batch=batch, seq=seq)
        merged = _gated_merge(h, o_a, o_b, w_gate, w_dsa_o[l].astype(BF16),
                              w_moba_o[l].astype(BF16), tm=PROJ_TM, tn=PROJ_TN)
        x1 = _matmul_residual(merged, w_out[l].astype(BF16), xf, tm=PROJ_TM, tn=PROJ_TN,
                              name="w_out_res")
        kv = _project(mem.reshape(batch * mem.shape[1], d), w_mem_kv[l].astype(BF16),
                      g=g_mem_kv[l], out_dtype=BF16, tm=256, tn=512, name="mem_kv")
        kv = kv.reshape(batch, mem.shape[1], 2 * MEM_W)
        x2 = _mem_attention(x1, g_mem_q[l], w_mem_q[l].astype(BF16), kv, w_mem_o[l].astype(BF16),
                            batch=batch, seq=seq)
        xf = _ffn_final(x2, g_ff[l], w_ff1[l].astype(BF16), w_ff2[l].astype(BF16), g_final)
    return xf.reshape(batch, seq, d)
```
